```python
import math
import jax, jax.numpy as jnp
from jax import lax
import numpy as np

D_MODEL = 4096
BATCH = 2
SEQ = 4096
DEPTH = 1
DEC_BATCH = 32
DEC_SEQ = 4
PAST_LEN = 8192
PAGE_SIZE = 128

HEAD_DIM = 128
N_HEADS = D_MODEL // HEAD_DIM
KV_HEADS = N_HEADS // 4
KV_REP = N_HEADS // KV_HEADS
MOBA_BLOCK = 256
MOBA_TOPK = 3
Q_CHUNK = 32
ROPE_THETA = 10000.0
SSM_INNER = D_MODEL
SSM_HEAD_DIM = 64
SSM_HEADS = SSM_INNER // SSM_HEAD_DIM
SSM_STATE = 128
SSM_GROUPS = 8
SSM_CHUNK = 128
CONV_W = 4
CONV_DIM = SSM_INNER + 2 * SSM_GROUPS * SSM_STATE
D_FF = 4 * D_MODEL
EPS = 1e-6
Q_DIM = N_HEADS * HEAD_DIM
KV_DIM = KV_HEADS * HEAD_DIM
IN_SPLITS = (Q_DIM, KV_DIM, KV_DIM, SSM_INNER, CONV_DIM, SSM_HEADS, D_MODEL, D_MODEL)
IN_DIM = Q_DIM + 2 * KV_DIM + SSM_INNER + CONV_DIM + SSM_HEADS + 2 * D_MODEL

kernel_name = 'hybrid_ssd_moba_adaln_step'


def rmsnorm(x, g):
    xf = x.astype(jnp.float32)
    y = xf * lax.rsqrt(jnp.mean(xf * xf, axis=-1, keepdims=True) + EPS)
    return (y * g.astype(jnp.float32)).astype(x.dtype)


def rope(x, pos):
    half = HEAD_DIM // 2
    inv = ROPE_THETA ** (-jnp.arange(half, dtype=jnp.float32) / half)
    ang = pos.astype(jnp.float32)[:, None] * inv[None, :]
    cos = jnp.cos(ang)[None, :, None, :]
    sin = jnp.sin(ang)[None, :, None, :]
    xf = x.astype(jnp.float32)
    x1, x2 = xf[..., :half], xf[..., half:]
    return jnp.concatenate([x1 * cos - x2 * sin, x2 * cos + x1 * sin], axis=-1).astype(x.dtype)


def split_cols(p):
    idx = [int(i) for i in np.cumsum(IN_SPLITS)[:-1]]
    return jnp.split(p, idx, axis=-1)


def causal_conv(xbc, prev, w, b):
    T = xbc.shape[1]
    full = jnp.concatenate([prev.astype(xbc.dtype), xbc], axis=1)
    out = b
    for j in range(CONV_W):
        out = out + full[:, j:j + T] * w[j]
    return out, full[:, full.shape[1] - (CONV_W - 1):]


def ssd_scan(x, dt, a, bm, cm, h0, chunk):
    B, T, H, P = x.shape
    nc = T // chunk
    R = H // SSM_GROUPS
    f32 = jnp.float32

    def to_chunks(t, tail):
        return t.astype(f32).reshape((B, nc, chunk) + tail).swapaxes(0, 1)

    xc = to_chunks(x, (SSM_GROUPS, R, P))
    dtc = to_chunks(dt, (SSM_GROUPS, R))
    bc = to_chunks(bm, (SSM_GROUPS, SSM_STATE))
    cc = to_chunks(cm, (SSM_GROUPS, SSM_STATE))
    ag = a.reshape(SSM_GROUPS, R)
    causal = jnp.tril(jnp.ones((chunk, chunk), dtype=bool))

    def step(h, inp):
        xk, dk, bk, ck = inp
        cum = jnp.cumsum(dk * ag, axis=1)
        seg = cum[:, :, None] - cum[:, None, :]
        decay = jnp.exp(jnp.where(causal[None, :, :, None, None], seg, -jnp.inf))
        cb = jnp.einsum('btgn,bsgn->btsg', ck, bk)
        wts = cb[..., None] * decay * dk[:, None]
        y = jnp.einsum('btsgr,bsgrp->btgrp', wts, xk)
        y = y + jnp.einsum('btgn,bgrpn->btgrp', ck, h) * jnp.exp(cum)[..., None]
        tail = jnp.exp(cum[:, -1:] - cum) * dk
        h = h * jnp.exp(cum[:, -1])[..., None, None] + jnp.einsum('bsgn,bsgr,bsgrp->bgrpn', bk, tail, xk)
        return h, y

    h, ys = lax.scan(step, h0.astype(f32).reshape(B, SSM_GROUPS, R, P, SSM_STATE), (xc, dtc, bc, cc))
    y = ys.swapaxes(0, 1).reshape(B, T, H, P)
    return y, h.reshape(B, H, P, SSM_STATE)


def moba_seq(q, k, v, pos):
    Tq, Tk = q.shape[0], k.shape[0]
    nb = -(-Tk // MOBA_BLOCK)
    pad = nb * MOBA_BLOCK - Tk
    k = jnp.pad(k, ((0, pad), (0, 0), (0, 0)))
    v = jnp.pad(v, ((0, pad), (0, 0), (0, 0)))
    kb = k.reshape(nb, MOBA_BLOCK, KV_HEADS, HEAD_DIM).transpose(2, 0, 1, 3)
    vb = v.reshape(nb, MOBA_BLOCK, KV_HEADS, HEAD_DIM).transpose(2, 0, 1, 3)
    kmean = jnp.mean(kb.astype(jnp.float32), axis=2)
    n_sel = min(MOBA_TOPK, nb)
    scale = HEAD_DIM ** -0.5
    qc = Q_CHUNK if Tq % Q_CHUNK == 0 else Tq
    qs = q.reshape(Tq // qc, qc, KV_HEADS, KV_REP, HEAD_DIM)
    ps = pos.reshape(Tq // qc, qc)
    f32 = jnp.float32

    def chunk(args):
        qq, pp = args
        own = pp // MOBA_BLOCK
        gate = jnp.einsum('qgrd,gnd->grqn', qq.astype(f32), kmean)
        past = jnp.arange(nb)[None, :] < own[:, None]
        gate = jnp.where(past[None, None], gate, -jnp.inf)
        _, idx = lax.top_k(gate, n_sel)
        ksel = jax.vmap(lambda blk, i: blk[i])(kb, idx)
        vsel = jax.vmap(lambda blk, i: blk[i])(vb, idx)
        kown = kb[:, own]
        vown = vb[:, own]
        s_sel = jnp.einsum('qgrd,grqsbd->grqsb', qq, ksel, preferred_element_type=f32) * scale
        sel_ok = jnp.arange(n_sel)[None, :] < own[:, None]
        s_sel = jnp.where(sel_ok[None, None, :, :, None], s_sel, -jnp.inf)
        s_own = jnp.einsum('qgrd,gqbd->grqb', qq, kown, preferred_element_type=f32) * scale
        kpos = own[:, None] * MOBA_BLOCK + jnp.arange(MOBA_BLOCK)[None, :]
        s_own = jnp.where((kpos <= pp[:, None])[None, None], s_own, -jnp.inf)
        s_all = jnp.concatenate([s_sel.reshape(KV_HEADS, KV_REP, qc, n_sel * MOBA_BLOCK), s_own], axis=-1)
        p = jax.nn.softmax(s_all, axis=-1).astype(v.dtype)
        p_sel = p[..., :n_sel * MOBA_BLOCK].reshape(KV_HEADS, KV_REP, qc, n_sel, MOBA_BLOCK)
        p_own = p[..., n_sel * MOBA_BLOCK:]
        o = jnp.einsum('grqsb,grqsbd->qgrd', p_sel, vsel) + jnp.einsum('grqb,gqbd->qgrd', p_own, vown)
        return o.reshape(qc, N_HEADS, HEAD_DIM)

    out = lax.map(chunk, (qs, ps))
    return out.reshape(Tq, N_HEADS, HEAD_DIM)


def block(x, c, pos, k_past, v_past, h0, conv_prev,
          w_ada, b_ada, g_norm1, w_in, q_norm_g, k_norm_g, w_conv, b_conv, dt_bias, a_log, d_skip,
          g_ssm, w_attn_o, w_ssm_o, w_out, g_norm2, w_up, w_down):
    B, T, _ = x.shape
    mod = (c @ w_ada + b_ada).reshape(B, 6, D_MODEL)[:, :, None, :]
    sh1, sc1, gt1, sh2, sc2, gt2 = (mod[:, i] for i in range(6))
    h = rmsnorm(x, g_norm1) * (1 + sc1) + sh1
    q, k, v, z, xbc, dt_raw, ga, gs = split_cols(h @ w_in)
    q = rope(rmsnorm(q.reshape(B, T, N_HEADS, HEAD_DIM), q_norm_g), pos)
    k = rope(rmsnorm(k.reshape(B, T, KV_HEADS, HEAD_DIM), k_norm_g), pos)
    v = v.reshape(B, T, KV_HEADS, HEAD_DIM)
    k_all = jnp.concatenate([k_past.astype(k.dtype), k], axis=1)
    v_all = jnp.concatenate([v_past.astype(v.dtype), v], axis=1)
    attn = lax.map(lambda a: moba_seq(a[0], a[1], a[2], pos), (q, k_all, v_all))
    attn = attn.reshape(B, T, Q_DIM) @ w_attn_o
    xbc, conv_new = causal_conv(xbc, conv_prev, w_conv, b_conv)
    xbc = jax.nn.silu(xbc)
    xs, bm, cm = jnp.split(xbc, [SSM_INNER, SSM_INNER + SSM_GROUPS * SSM_STATE], axis=-1)
    dt = jax.nn.softplus(dt_raw.astype(jnp.float32) + dt_bias.astype(jnp.float32))
    a = -jnp.exp(a_log.astype(jnp.float32))
    xs_h = xs.reshape(B, T, SSM_HEADS, SSM_HEAD_DIM)
    chunk = SSM_CHUNK if T % SSM_CHUNK == 0 else T
    y, h_fin = ssd_scan(xs_h, dt, a, bm.reshape(B, T, SSM_GROUPS, SSM_STATE),
                        cm.reshape(B, T, SSM_GROUPS, SSM_STATE), h0, chunk)
    y = y + d_skip.astype(jnp.float32)[:, None] * xs_h.astype(jnp.float32)
    y = y.astype(x.dtype).reshape(B, T, SSM_INNER) * jax.nn.silu(z)
    y = rmsnorm(y.reshape(B, T, SSM_GROUPS, SSM_INNER // SSM_GROUPS),
                g_ssm.reshape(SSM_GROUPS, SSM_INNER // SSM_GROUPS)).reshape(B, T, SSM_INNER)
    ssm = y @ w_ssm_o
    mix = (jax.nn.sigmoid(ga) * attn + jax.nn.sigmoid(gs) * ssm) @ w_out
    x = x + gt1 * mix
    h2 = rmsnorm(x, g_norm2) * (1 + sc2) + sh2
    x = x + gt2 * (jnp.square(jax.nn.relu(h2 @ w_up)) @ w_down)
    return x, k, v, h_fin.astype(x.dtype), conv_new


def setup_inputs(seed: int = 0) -> dict:
    key = jax.random.key(seed)
    ks = jax.random.split(key, 32)
    f32 = jnp.float32

    def nrm(k, shape, s):
        return jax.random.normal(k, shape, f32) * s

    n_pages = PAST_LEN // PAGE_SIZE
    n_used = DEC_BATCH * n_pages
    n_pool = n_used + max(1, n_used // 4)
    page_table = jax.random.permutation(ks[0], n_pool)[:n_used].reshape(DEC_BATCH, n_pages).astype(jnp.int32)
    dt0 = jnp.exp(jax.random.uniform(ks[1], (SSM_HEADS,), f32, math.log(1e-3), math.log(1e-1)))
    return {
        'x_prompt': nrm(ks[2], (BATCH, SEQ, D_MODEL), 1.0),
        'x_sample': nrm(ks[3], (DEC_BATCH, DEC_SEQ, D_MODEL), 1.0),
        'cache_k': nrm(ks[4], (n_pool, PAGE_SIZE, KV_HEADS, HEAD_DIM), 1.0),
        'cache_v': nrm(ks[5], (n_pool, PAGE_SIZE, KV_HEADS, HEAD_DIM), 1.0),
        'state_ssm': nrm(ks[6], (DEC_BATCH, SSM_HEADS, SSM_HEAD_DIM, SSM_STATE), 0.1),
        'state_conv': nrm(ks[7], (DEC_BATCH, CONV_W - 1, CONV_DIM), 1.0),
        'page_table': page_table,
        'c_prompt': nrm(ks[8], (BATCH, D_MODEL), 1.0),
        'c_sample': nrm(ks[9], (DEC_BATCH, D_MODEL), 1.0),
        'w_ada': nrm(ks[10], (D_MODEL, 6 * D_MODEL), 0.3 * D_MODEL ** -0.5),
        'b_ada': nrm(ks[11], (6 * D_MODEL,), 0.01),
        'g_norm1': 1.0 + nrm(ks[12], (D_MODEL,), 0.01),
        'w_in': nrm(ks[13], (D_MODEL, IN_DIM), D_MODEL ** -0.5),
        'q_norm_g': 1.0 + nrm(ks[14], (HEAD_DIM,), 0.01),
        'k_norm_g': 1.0 + nrm(ks[15], (HEAD_DIM,), 0.01),
        'w_conv': nrm(ks[16], (CONV_W, CONV_DIM), CONV_W ** -0.5),
        'b_conv': nrm(ks[17], (CONV_DIM,), 0.01),
        'dt_bias': dt0 + jnp.log(-jnp.expm1(-dt0)),
        'a_log': jnp.log(jax.random.uniform(ks[18], (SSM_HEADS,), f32, 1.0, 16.0)),
        'd_skip': 1.0 + nrm(ks[19], (SSM_HEADS,), 0.1),
        'g_ssm': 1.0 + nrm(ks[20], (SSM_INNER,), 0.01),
        'w_attn_o': nrm(ks[21], (Q_DIM, D_MODEL), Q_DIM ** -0.5),
        'w_ssm_o': nrm(ks[22], (SSM_INNER, D_MODEL), SSM_INNER ** -0.5),
        'w_out': nrm(ks[23], (D_MODEL, D_MODEL), D_MODEL ** -0.5),
        'g_norm2': 1.0 + nrm(ks[24], (D_MODEL,), 0.01),
        'w_up': nrm(ks[25], (D_MODEL, D_FF), D_MODEL ** -0.5),
        'w_down': nrm(ks[26], (D_FF, D_MODEL), D_FF ** -0.5),
    }


def reference(x_prompt, x_sample, cache_k, cache_v, state_ssm, state_conv, page_table, c_prompt, c_sample,
              w_ada, b_ada, g_norm1, w_in, q_norm_g, k_norm_g, w_conv, b_conv, dt_bias, a_log, d_skip,
              g_ssm, w_attn_o, w_ssm_o, w_out, g_norm2, w_up, w_down):
    weights = (w_ada, b_ada, g_norm1, w_in, q_norm_g, k_norm_g, w_conv, b_conv, dt_bias, a_log, d_skip,
               g_ssm, w_attn_o, w_ssm_o, w_out, g_norm2, w_up, w_down)
    bp, tp, _ = x_prompt.shape
    bs, ts, _ = x_sample.shape
    past_len = page_table.shape[1] * cache_k.shape[1]
    pos_p = jnp.arange(tp, dtype=jnp.int32)
    pos_s = past_len + jnp.arange(ts, dtype=jnp.int32)
    k_past = cache_k[page_table].reshape(bs, past_len, KV_HEADS, HEAD_DIM)
    v_past = cache_v[page_table].reshape(bs, past_len, KV_HEADS, HEAD_DIM)
    empty_kv = jnp.zeros((bp, 0, KV_HEADS, HEAD_DIM), x_prompt.dtype)
    h0_p = jnp.zeros((bp, SSM_HEADS, SSM_HEAD_DIM, SSM_STATE), x_prompt.dtype)
    conv0_p = jnp.zeros((bp, CONV_W - 1, CONV_DIM), x_prompt.dtype)
    y_prompt, y_sample = x_prompt, x_sample
    for _ in range(DEPTH):
        y_prompt, k_p, v_p, ssm_p, conv_p = block(y_prompt, c_prompt, pos_p, empty_kv, empty_kv, h0_p, conv0_p, *weights)
        y_sample, k_s, v_s, ssm_s, conv_s = block(y_sample, c_sample, pos_s, k_past, v_past, state_ssm, state_conv, *weights)
    return (y_prompt, y_sample, k_p, v_p, ssm_p, conv_p, k_s, v_s, ssm_s, conv_s)
```

```python
import functools

import jax
import jax.numpy as jnp
from jax import lax
from jax.experimental import pallas as pl
from jax.experimental.pallas import tpu as pltpu

F32, BF16 = jnp.float32, jnp.bfloat16

D_MODEL = 4096
HEAD_DIM = 128
N_HEADS = 32
KV_HEADS = 8
KV_REP = 4
MOBA_BLOCK = 256
MOBA_TOPK = 3
ROPE_THETA = 10000.0
PAGE_SIZE = 128
SSM_HEADS = 64
SSM_HEAD_DIM = 64
SSM_STATE = 128
SSM_GROUPS = 8
SSM_REP = 8
SSM_INNER = 4096
SSM_CHUNK = 128
CONV_W = 4
CONV_DIM = 6144
D_FF = 16384
EPS = 1e-6
Q_DIM = N_HEADS * HEAD_DIM
KV_DIM = KV_HEADS * HEAD_DIM
ATTN_SCALE = HEAD_DIM ** -0.5
MASK_SCORE = -1e30
VMEM_LIMIT_BYTES = 56 * 1024 * 1024
LANES = 128
DEC_ROWS = 128


def _params(n_axes):
    return pltpu.CompilerParams(dimension_semantics=("arbitrary",) * n_axes,
                                vmem_limit_bytes=VMEM_LIMIT_BYTES)


def _split3(x):
    hi = x.astype(BF16)
    r1 = x - hi.astype(F32)
    mid = r1.astype(BF16)
    lo = (r1 - mid.astype(F32)).astype(BF16)
    return hi, mid, lo


def _dot3_lhs_exact(w_exact, x):
    hi, mid, lo = _split3(x)
    f = functools.partial(jnp.dot, preferred_element_type=F32)
    return f(w_exact, lo) + f(w_exact, mid) + f(w_exact, hi)


def _dot3_rhs_exact(x, w_exact):
    hi, mid, lo = _split3(x)
    f = functools.partial(jnp.dot, preferred_element_type=F32)
    return f(lo, w_exact) + f(mid, w_exact) + f(hi, w_exact)


def _dot_hp(a, b, dims):
    a0, a1, a2 = _split3(a)
    b0, b1, b2 = _split3(b)
    f = functools.partial(lax.dot_general, dimension_numbers=dims, preferred_element_type=F32)
    return (f(a2, b0) + f(a1, b1) + f(a0, b2)) + (f(a1, b0) + f(a0, b1)) + f(a0, b0)


def _norm_rope(xh, g, c, s):
    xn = xh * lax.rsqrt(jnp.mean(xh * xh, axis=-1, keepdims=True) + EPS) * g
    return xn * c + pltpu.roll(xn, HEAD_DIM // 2, 1) * s


def _topk_select(gate, n_idx, n_blocks):
    cnt = jnp.zeros(gate.shape, F32)
    for m in range(n_blocks):
        rm = gate[m:m + 1, :]
        beats = jnp.where(rm > gate, 1.0, jnp.where((rm == gate) & (n_idx > m), 1.0, 0.0))
        cnt = cnt + beats
    return cnt < MOBA_TOPK


def _rope_tables(pos):
    half = HEAD_DIM // 2
    inv = ROPE_THETA ** (-jnp.arange(half, dtype=F32) / half)
    ang = pos.astype(F32)[:, None] * inv[None, :]
    c, s = jnp.cos(ang), jnp.sin(ang)
    return jnp.concatenate([c, c], axis=-1), jnp.concatenate([-s, s], axis=-1)


def _epi_store(acc, rows, is_sample, ex, o_ref):
    o_ref[:rows, :] = acc.astype(o_ref.dtype)


def _epi_bias(acc, rows, is_sample, ex, o_ref):
    o_ref[:rows, :] = acc + ex[0][...]


def _epi_sigmoid(acc, rows, is_sample, ex, o_ref):
    o_ref[:rows, :] = jax.nn.sigmoid(acc)


def _epi_relu2(acc, rows, is_sample, ex, o_ref):
    r = jnp.maximum(acc, 0.0)
    o_ref[:rows, :] = (r * r).astype(o_ref.dtype)


def _epi_residual(acc, rows, is_sample, ex, o_ref):
    x_ref, gate_p_ref, gate_s_ref = ex
    gate = gate_s_ref[...] if is_sample else gate_p_ref[0]
    o_ref[:rows, :] = x_ref[:rows, :] + gate * acc


def _mm_kernel(a_ref, w_ref, *refs, n_ex, nk, n_i, tm, rows_last, epi):
    ex, o_ref, scr = refs[:n_ex], refs[n_ex], refs[n_ex + 1:]
    i, k = pl.program_id(1), pl.program_id(2)
    if nk == 1:
        wb_ref, = scr

        @pl.when(i == 0)
        def _():
            wb_ref[...] = w_ref[...].astype(BF16)

        def run(rows, is_sample):
            acc = jnp.dot(a_ref[:rows, :], wb_ref[...], preferred_element_type=F32)
            epi(acc, rows, is_sample, ex, o_ref)
    else:
        acc_ref, = scr

        def run(rows, is_sample):
            part = jnp.dot(a_ref[:rows, :], w_ref[...].astype(BF16), preferred_element_type=F32)

            @pl.when(k == 0)
            def _():
                acc_ref[:rows, :] = part

            @pl.when(k > 0)
            def _():
                acc_ref[:rows, :] += part

            @pl.when(k == nk - 1)
            def _():
                epi(acc_ref[:rows, :], rows, is_sample, ex, o_ref)

    if rows_last == tm:
        run(tm, False)
    else:
        pl.when(i < n_i - 1)(lambda: run(tm, False))
        pl.when(i == n_i - 1)(lambda: run(rows_last, True))


def _mm(a, w, *, n_cols, col_off=0, tm, tn, tk, out_dtype, epi, extras=(), extra_specs=(), name):
    m, kdim = a.shape
    nk, n_i, n_j = kdim // tk, pl.cdiv(m, tm), n_cols // tn
    rows_last = m - (n_i - 1) * tm
    off = col_off // tn
    in_specs = [pl.BlockSpec((tm, tk), lambda j, i, k: (i, k)),
                pl.BlockSpec((tk, tn), lambda j, i, k: (k, j + off))] + list(extra_specs)
    scratch = [pltpu.VMEM((tk, tn), BF16)] if nk == 1 else [pltpu.VMEM((tm, tn), F32)]
    kern = functools.partial(_mm_kernel, n_ex=len(extras), nk=nk, n_i=n_i, tm=tm, rows_last=rows_last, epi=epi)
    return pl.pallas_call(
        kern, grid=(n_j, n_i, nk), in_specs=in_specs,
        out_specs=pl.BlockSpec((tm, tn), lambda j, i, k: (i, j)),
        out_shape=jax.ShapeDtypeStruct((m, n_cols), out_dtype),
        scratch_shapes=scratch, compiler_params=_params(3), name=name)(a, w, *extras)


def _residual_specs(tm, tn, rows_per_batch, n_batch):
    tiles_per_batch = rows_per_batch // tm
    return [pl.BlockSpec((tm, tn), lambda j, i, k: (i, j)),
            pl.BlockSpec((1, 1, tn), lambda j, i, k: (jnp.minimum(i // tiles_per_batch, n_batch - 1), 0, j)),
            pl.BlockSpec((DEC_ROWS, tn), lambda j, i, k: (0, j))]


def _mix_kernel(a1_ref, a2_ref, w1_ref, w2_ref, g1_ref, g2_ref, o_ref, wb1_ref, wb2_ref, *, n_i, tm, rows_last):
    i = pl.program_id(1)

    @pl.when(i == 0)
    def _():
        wb1_ref[...] = w1_ref[...].astype(BF16)
        wb2_ref[...] = w2_ref[...].astype(BF16)

    def run(rows):
        o1 = jnp.dot(a1_ref[:rows, :], wb1_ref[...], preferred_element_type=F32)
        o2 = jnp.dot(a2_ref[:rows, :], wb2_ref[...], preferred_element_type=F32)
        o_ref[:rows, :] = (g1_ref[:rows, :] * o1 + g2_ref[:rows, :] * o2).astype(o_ref.dtype)

    pl.when(i < n_i - 1)(lambda: run(tm))
    pl.when(i == n_i - 1)(lambda: run(rows_last))


def _mix(attn, ssm, w_attn_o, w_ssm_o, gates, *, tm, tn):
    m, kdim = attn.shape
    n_i, n_j = pl.cdiv(m, tm), D_MODEL // tn
    rows_last = m - (n_i - 1) * tm
    goff = D_MODEL // tn
    kern = functools.partial(_mix_kernel, n_i=n_i, tm=tm, rows_last=rows_last)
    return pl.pallas_call(
        kern, grid=(n_j, n_i),
        in_specs=[pl.BlockSpec((tm, kdim), lambda j, i: (i, 0)),
                  pl.BlockSpec((tm, kdim), lambda j, i: (i, 0)),
                  pl.BlockSpec((kdim, tn), lambda j, i: (0, j)),
                  pl.BlockSpec((kdim, tn), lambda j, i: (0, j)),
                  pl.BlockSpec((tm, tn), lambda j, i: (i, j)),
                  pl.BlockSpec((tm, tn), lambda j, i: (i, j + goff))],
        out_specs=pl.BlockSpec((tm, tn), lambda j, i: (i, j)),
        out_shape=jax.ShapeDtypeStruct((m, D_MODEL), BF16),
        scratch_shapes=[pltpu.VMEM((kdim, tn), BF16), pltpu.VMEM((kdim, tn), BF16)],
        compiler_params=_params(2), name="mix_proj")(attn, ssm, w_attn_o, w_ssm_o, gates, gates)


def _norm_mod_kernel(x_ref, g_ref, scp_ref, shp_ref, scs_ref, shs_ref, o_ref, *, n_i):
    is_sample = pl.program_id(0) == n_i - 1
    x = x_ref[...]
    y = x * lax.rsqrt(jnp.mean(x * x, axis=-1, keepdims=True) + EPS) * g_ref[...]
    sc = jnp.where(is_sample, scs_ref[...], scp_ref[0])
    sh = jnp.where(is_sample, shs_ref[...], shp_ref[0])
    o_ref[...] = (y * (1.0 + sc) + sh).astype(o_ref.dtype)


def _norm_mod(x, g, sc_p, sh_p, sc_s, sh_s, *, rows_per_batch, name):
    m, d = x.shape
    tm = DEC_ROWS
    n_i = m // tm
    tpb = rows_per_batch // tm
    nb = sc_p.shape[0]
    pspec = pl.BlockSpec((1, 1, d), lambda i: (jnp.minimum(i // tpb, nb - 1), 0, 0))
    sspec = pl.BlockSpec((tm, d), lambda i: (0, 0))
    return pl.pallas_call(
        functools.partial(_norm_mod_kernel, n_i=n_i), grid=(n_i,),
        in_specs=[pl.BlockSpec((tm, d), lambda i: (i, 0)), pl.BlockSpec((1, d), lambda i: (0, 0)),
                  pspec, pspec, sspec, sspec],
        out_specs=pl.BlockSpec((tm, d), lambda i: (i, 0)),
        out_shape=jax.ShapeDtypeStruct((m, d), BF16),
        compiler_params=_params(1), name=name)(x, g, sc_p, sh_p, sc_s, sh_s)


def _norm_rope_kernel(x_ref, c_ref, s_ref, g_ref, o_ref, *, n_heads):
    c, s, g = c_ref[...], s_ref[...], g_ref[...]
    for h in range(n_heads):
        sl = slice(h * HEAD_DIM, (h + 1) * HEAD_DIM)
        o_ref[:, sl] = _norm_rope(x_ref[:, sl], g, c, s)


def _norm_rope_rows(x, row_block, c, s, g, *, n_heads, name):
    w = n_heads * HEAD_DIM
    return pl.pallas_call(
        functools.partial(_norm_rope_kernel, n_heads=n_heads), grid=(1,),
        in_specs=[pl.BlockSpec((DEC_ROWS, w), lambda i: (row_block, 0)),
                  pl.BlockSpec((DEC_ROWS, HEAD_DIM), lambda i: (0, 0)),
                  pl.BlockSpec((DEC_ROWS, HEAD_DIM), lambda i: (0, 0)),
                  pl.BlockSpec((1, HEAD_DIM), lambda i: (0, 0))],
        out_specs=pl.BlockSpec((DEC_ROWS, w), lambda i: (0, 0)),
        out_shape=jax.ShapeDtypeStruct((DEC_ROWS, w), F32),
        compiler_params=_params(1), name=name)(x, c, s, g)


def _k_prep_kernel(k_ref, v_ref, c_ref, s_ref, g_ref, ko_ref, ka_ref, va_ref, km_ref, *, n_blk):
    blk = pl.program_id(0) % n_blk
    c, s, g = c_ref[...], s_ref[...], g_ref[...]
    lane = lax.broadcasted_iota(jnp.int32, (MOBA_BLOCK, HEAD_DIM), 1)
    block_tag = jnp.where(lane == blk, 1.0, 0.0).astype(BF16)
    ones_col = jnp.where(lane == 0, 1.0, 0.0).astype(BF16)
    for h in range(KV_HEADS):
        sl = slice(h * HEAD_DIM, (h + 1) * HEAD_DIM)
        lo, mid, hi = 2 * h * HEAD_DIM, (2 * h + 1) * HEAD_DIM, (2 * h + 2) * HEAD_DIM
        kr = _norm_rope(k_ref[:, sl], g, c, s)
        ko_ref[:, sl] = kr
        ka_ref[:, lo:mid] = kr.astype(BF16)
        ka_ref[:, mid:hi] = block_tag
        va_ref[:, lo:mid] = v_ref[:, sl].astype(BF16)
        va_ref[:, mid:hi] = ones_col
        km_ref[0, :, sl] = jnp.mean(kr, axis=0, keepdims=True)


def _k_prep(kv, c, s, g, *, n_rows, n_blk):
    nt = n_rows // MOBA_BLOCK
    return pl.pallas_call(
        functools.partial(_k_prep_kernel, n_blk=n_blk), grid=(nt,),
        in_specs=[pl.BlockSpec((MOBA_BLOCK, KV_DIM), lambda t: (t, 0)),
                  pl.BlockSpec((MOBA_BLOCK, KV_DIM), lambda t: (t, 1)),
                  pl.BlockSpec((MOBA_BLOCK, HEAD_DIM), lambda t: (t % n_blk, 0)),
                  pl.BlockSpec((MOBA_BLOCK, HEAD_DIM), lambda t: (t % n_blk, 0)),
                  pl.BlockSpec((1, HEAD_DIM), lambda t: (0, 0))],
        out_specs=[pl.BlockSpec((MOBA_BLOCK, KV_DIM), lambda t: (t, 0)),
                   pl.BlockSpec((MOBA_BLOCK, 2 * KV_DIM), lambda t: (t, 0)),
                   pl.BlockSpec((MOBA_BLOCK, 2 * KV_DIM), lambda t: (t, 0)),
                   pl.BlockSpec((1, 1, KV_DIM), lambda t: (t, 0, 0))],
        out_shape=[jax.ShapeDtypeStruct((n_rows, KV_DIM), F32),
                   jax.ShapeDtypeStruct((n_rows, 2 * KV_DIM), BF16),
                   jax.ShapeDtypeStruct((n_rows, 2 * KV_DIM), BF16),
                   jax.ShapeDtypeStruct((nt, 1, KV_DIM), F32)],
        compiler_params=_params(1), name="k_prep")(kv, kv, c, s, g)


def _q_prep_kernel(q_ref, c_ref, s_ref, g_ref, km_ref, o_ref, *, n_blk):
    own = pl.program_id(0) % n_blk
    c, s, g = c_ref[...], s_ref[...], g_ref[...]
    n_idx = lax.broadcasted_iota(jnp.int32, (n_blk, MOBA_BLOCK), 0)
    lane = lax.broadcasted_iota(jnp.int32, (MOBA_BLOCK, HEAD_DIM), 1)
    past = n_idx < own
    km = km_ref[0]
    pad = jnp.zeros((HEAD_DIM - n_blk, MOBA_BLOCK), F32)
    for h in range(N_HEADS):
        kv = h // KV_REP
        qh = _norm_rope(q_ref[:, h * HEAD_DIM:(h + 1) * HEAD_DIM], g, c, s)
        gate = _dot_hp(km[:, kv * HEAD_DIM:(kv + 1) * HEAD_DIM], qh, (((1,), (1,)), ((), ())))
        gate = jnp.where(past, gate, -jnp.inf)
        sel = (_topk_select(gate, n_idx, n_blk) & past) | (n_idx == own)
        sel_t = jnp.concatenate([jnp.where(sel, 1.0, 0.0), pad], axis=0).T
        bias = jnp.where(lane < n_blk, (sel_t - 1.0) * (-MASK_SCORE), 0.0)
        o_ref[:, 2 * h * HEAD_DIM:(2 * h + 1) * HEAD_DIM] = (qh * ATTN_SCALE).astype(BF16)
        o_ref[:, (2 * h + 1) * HEAD_DIM:(2 * h + 2) * HEAD_DIM] = bias.astype(BF16)


def _q_prep(q, c, s, g, kmean, *, n_rows, n_blk):
    nt = n_rows // MOBA_BLOCK
    return pl.pallas_call(
        functools.partial(_q_prep_kernel, n_blk=n_blk), grid=(nt,),
        in_specs=[pl.BlockSpec((MOBA_BLOCK, Q_DIM), lambda t: (t, 0)),
                  pl.BlockSpec((MOBA_BLOCK, HEAD_DIM), lambda t: (t % n_blk, 0)),
                  pl.BlockSpec((MOBA_BLOCK, HEAD_DIM), lambda t: (t % n_blk, 0)),
                  pl.BlockSpec((1, HEAD_DIM), lambda t: (0, 0)),
                  pl.BlockSpec((1, n_blk, KV_DIM), lambda t: (t // n_blk, 0, 0))],
        out_specs=pl.BlockSpec((MOBA_BLOCK, 2 * Q_DIM), lambda t: (t, 0)),
        out_shape=jax.ShapeDtypeStruct((n_rows, 2 * Q_DIM), BF16),
        compiler_params=_params(1), name="q_prep")(q, c, s, g, kmean)


def _attn_p_kernel(q_ref, k_ref, v_ref, o_ref):
    i = pl.program_id(2)
    rows = KV_REP * MOBA_BLOCK
    aug = 2 * HEAD_DIM
    qs = jnp.concatenate([q_ref[:, r * aug:(r + 1) * aug] for r in range(KV_REP)], axis=0)
    dims = (((1,), (1,)), ((), ()))
    row = lax.broadcasted_iota(jnp.int32, (rows, MOBA_BLOCK), 0) & (MOBA_BLOCK - 1)
    col = lax.broadcasted_iota(jnp.int32, (rows, MOBA_BLOCK), 1)

    st = pl.multiple_of(i * MOBA_BLOCK, MOBA_BLOCK)
    s = lax.dot_general(qs, k_ref[pl.ds(st, MOBA_BLOCK), :], dims, preferred_element_type=F32)
    s = jnp.where(col <= row, s, -jnp.inf)
    m = jnp.max(s, axis=1, keepdims=True)
    p = jnp.exp(s - m)
    acc = jnp.dot(p.astype(BF16), v_ref[pl.ds(st, MOBA_BLOCK), :], preferred_element_type=F32)

    def body(j, carry):
        m, acc = carry
        sj = pl.multiple_of(j * MOBA_BLOCK, MOBA_BLOCK)
        s = lax.dot_general(qs, k_ref[pl.ds(sj, MOBA_BLOCK), :], dims, preferred_element_type=F32)
        m_new = jnp.maximum(m, jnp.max(s, axis=1, keepdims=True))
        p = jnp.exp(s - m_new)
        acc = acc * jnp.exp(m - m_new) + jnp.dot(p.astype(BF16), v_ref[pl.ds(sj, MOBA_BLOCK), :],
                                                  preferred_element_type=F32)
        return m_new, acc

    m, acc = lax.fori_loop(0, i, body, (m, acc))
    o = acc[:, :HEAD_DIM] / acc[:, HEAD_DIM:HEAD_DIM + 1]
    for r in range(KV_REP):
        o_ref[:, r * HEAD_DIM:(r + 1) * HEAD_DIM] = o[r * MOBA_BLOCK:(r + 1) * MOBA_BLOCK].astype(o_ref.dtype)


def _attn_prompt(q_aug, k_aug, v_aug, *, n_batch, seq):
    n_blk = seq // MOBA_BLOCK
    aug = 2 * HEAD_DIM
    return pl.pallas_call(
        _attn_p_kernel, grid=(n_batch, KV_HEADS, n_blk),
        in_specs=[pl.BlockSpec((MOBA_BLOCK, KV_REP * aug), lambda b, g, i: (b * n_blk + i, g)),
                  pl.BlockSpec((seq, aug), lambda b, g, i: (b, g)),
                  pl.BlockSpec((seq, aug), lambda b, g, i: (b, g))],
        out_specs=pl.BlockSpec((MOBA_BLOCK, KV_REP * HEAD_DIM), lambda b, g, i: (b * n_blk + i, g)),
        out_shape=jax.ShapeDtypeStruct((n_batch * seq, Q_DIM), BF16),
        compiler_params=_params(3), name="attn_prompt")(q_aug, k_aug, v_aug)


GROUP_ROWS = 16


def _diag_blocks(full):
    return jnp.concatenate([full[g * GROUP_ROWS:(g + 1) * GROUP_ROWS, g * HEAD_DIM:(g + 1) * HEAD_DIM]
                            for g in range(KV_HEADS)], axis=0)


def _attn_s_kernel(pt_ref, k0_ref, k1_ref, v0_ref, v1_ref, qb_ref, qf_ref, ko_ref, vo_ref, o_ref,
                   ob_ref, m_ref, l_ref, ks_ref, *, n_blk, dec_seq):
    jb = pl.program_id(1)
    kblk = jnp.concatenate([k0_ref[0], k1_ref[0]], axis=0)
    vblk = jnp.concatenate([v0_ref[0], v1_ref[0]], axis=0).astype(BF16)
    qb = qb_ref[0]
    ks_ref[pl.ds(jb, 1), :] = jnp.sum(kblk, axis=0, keepdims=True)
    st = jnp.dot(kblk.astype(BF16), qb, preferred_element_type=F32)
    mb = jnp.max(st, axis=0, keepdims=True)
    p = jnp.exp(st - mb)
    m_ref[pl.ds(jb, 1), :] = mb
    l_ref[pl.ds(jb, 1), :] = jnp.sum(p, axis=0, keepdims=True)
    ob_ref[jb] = _diag_blocks(jnp.dot(p.T.astype(BF16), vblk, preferred_element_type=F32))

    @pl.when(jb == n_blk - 1)
    def _():
        gate = _dot_hp(ks_ref[...] * (1.0 / MOBA_BLOCK), qf_ref[0], (((1,), (0,)), ((), ())))
        n_idx = lax.broadcasted_iota(jnp.int32, (n_blk, LANES), 0)
        sel = _topk_select(gate, n_idx, n_blk)
        m_all = m_ref[...]
        so = jnp.dot(ko_ref[0].astype(BF16), qb, preferred_element_type=F32)
        krow = lax.broadcasted_iota(jnp.int32, so.shape, 0)
        tq = (lax.broadcasted_iota(jnp.int32, so.shape, 1) >> 2) & (dec_seq - 1)
        so = jnp.where((krow <= tq) & (krow < dec_seq), so, -jnp.inf)
        m_tot = jnp.maximum(jnp.max(jnp.where(sel, m_all, -jnp.inf), axis=0, keepdims=True),
                            jnp.max(so, axis=0, keepdims=True))
        w = jnp.where(sel, jnp.exp(m_all - m_tot), 0.0)
        po = jnp.exp(so - m_tot)
        inv = 1.0 / (jnp.sum(w * l_ref[...], axis=0, keepdims=True) + jnp.sum(po, axis=0, keepdims=True))
        pad = jnp.zeros((LANES - n_blk - so.shape[0], LANES), F32)
        wt = jnp.concatenate([w * inv, po * inv, pad], axis=0).T
        acc = jnp.zeros((LANES, HEAD_DIM), F32)
        for b in range(n_blk):
            acc = acc + wt[:, b:b + 1] * ob_ref[b]
        vo = vo_ref[0]
        for t in range(dec_seq):
            vrow = jnp.concatenate(
                [jnp.broadcast_to(vo[t:t + 1, g * HEAD_DIM:(g + 1) * HEAD_DIM], (GROUP_ROWS, HEAD_DIM))
                 for g in range(KV_HEADS)], axis=0)
            acc = acc + wt[:, n_blk + t:n_blk + t + 1] * vrow
        o_ref[0] = acc


def _attn_sample(page_table, cache_k, cache_v, qbd, qbd_f32, k_own, v_own, *, dec_seq):
    n_seq, n_pages = page_table.shape
    n_blk = n_pages * PAGE_SIZE // MOBA_BLOCK
    page = pl.BlockSpec((1, PAGE_SIZE, KV_DIM), lambda s, jb, pt: (pt[s, 2 * jb], 0, 0))
    page1 = pl.BlockSpec((1, PAGE_SIZE, KV_DIM), lambda s, jb, pt: (pt[s, 2 * jb + 1], 0, 0))
    per_seq = lambda r, c: pl.BlockSpec((1, r, c), lambda s, jb, pt: (s, 0, 0))
    grid_spec = pltpu.PrefetchScalarGridSpec(
        num_scalar_prefetch=1, grid=(n_seq, n_blk),
        in_specs=[page, page1, page, page1, per_seq(KV_DIM, LANES), per_seq(KV_DIM, LANES),
                  per_seq(8, KV_DIM), per_seq(8, KV_DIM)],
        out_specs=per_seq(LANES, HEAD_DIM),
        scratch_shapes=[pltpu.VMEM((n_blk, LANES, HEAD_DIM), F32), pltpu.VMEM((n_blk, LANES), F32),
                        pltpu.VMEM((n_blk, LANES), F32), pltpu.VMEM((n_blk, KV_DIM), F32)])
    return pl.pallas_call(
        functools.partial(_attn_s_kernel, n_blk=n_blk, dec_seq=dec_seq), grid_spec=grid_spec,
        out_shape=jax.ShapeDtypeStruct((n_seq, LANES, HEAD_DIM), F32),
        compiler_params=_params(2), name="attn_sample")(
            page_table, cache_k, cache_k, cache_v, cache_v, qbd, qbd_f32, k_own, v_own)


CONV_PAD = 8


def _ssd_kernel(xbc_ref, z_ref, dt_ref, cprev_ref, h0_ref, wc_ref, bc_ref, dtb_ref, alog_ref, dx_ref, gs_ref,
                e2_ref, e1_ref, tril_ref, y_ref, hf_ref, cnew_ref, ht_ref, cbuf_ref, *, L, nc, n_valid, rows_in):
    c = pl.program_id(1)
    gw = SSM_REP * SSM_HEAD_DIM

    @pl.when(c == 0)
    def _():
        cbuf_ref[0:CONV_PAD, :] = cprev_ref[0]
        if rows_in < L:
            cbuf_ref[CONV_PAD + rows_in:, :] = jnp.zeros((L - rows_in, CONV_DIM), F32)
        for g in range(SSM_GROUPS):
            ht_ref[g] = h0_ref[0, g * SSM_REP:(g + 1) * SSM_REP].reshape(gw, SSM_STATE).T

    cbuf_ref[CONV_PAD:CONV_PAD + rows_in, :] = xbc_ref[...]
    first = CONV_PAD - (CONV_W - 1)
    conv = bc_ref[...]
    for j in range(CONV_W):
        conv = conv + cbuf_ref[first + j:first + j + L, :] * wc_ref[j:j + 1, :]
    cnew_ref[0] = cbuf_ref[n_valid:CONV_PAD + n_valid, :]
    cbuf_ref[first:CONV_PAD, :] = cbuf_ref[first + L:CONV_PAD + L, :]

    act = conv * jax.nn.sigmoid(conv)
    xs = act[:, :SSM_INNER]
    if rows_in < L:
        zpad = jnp.zeros((L - rows_in, SSM_INNER), F32)
        z = jnp.concatenate([z_ref[...], zpad], axis=0)
        dt_raw = jnp.concatenate([dt_ref[...], jnp.zeros((L - rows_in, LANES), F32)], axis=0)
    else:
        z, dt_raw = z_ref[...], dt_ref[...]

    row = lax.broadcasted_iota(jnp.int32, (L, LANES), 0)
    lane = lax.broadcasted_iota(jnp.int32, (L, LANES), 1)
    dt = jax.nn.softplus(dt_raw + dtb_ref[...])
    dt = jnp.where((row < n_valid) & (lane < SSM_HEADS), dt, 0.0)
    cum = _dot3_lhs_exact(tril_ref[...], dt * (-jnp.exp(alog_ref[...])))
    cum_t, dt_t = cum.T, dt.T
    causal = lax.broadcasted_iota(jnp.int32, (L, L), 1) <= lax.broadcasted_iota(jnp.int32, (L, L), 0)

    outs = []
    for g in range(SSM_GROUPS):
        sl = slice(g * gw, (g + 1) * gw)
        bg = act[:, SSM_INNER + g * SSM_STATE:SSM_INNER + (g + 1) * SSM_STATE]
        cg = act[:, SSM_INNER + (SSM_GROUPS + g) * SSM_STATE:SSM_INNER + (SSM_GROUPS + g + 1) * SSM_STATE]
        bg_b, cg_b = bg.astype(BF16), cg.astype(BF16)
        cb = lax.dot_general(cg_b, bg_b, (((1,), (1,)), ((), ())), preferred_element_type=F32)
        cum_x2 = _dot3_rhs_exact(cum, e2_ref[:, g * SSM_REP * LANES:(g + 1) * SSM_REP * LANES])
        cum_x = _dot3_rhs_exact(cum, e1_ref[:, sl])
        dt_x = _dot3_rhs_exact(dt, e1_ref[:, sl])
        ht = ht_ref[g]
        y_inter = jnp.dot(cg_b, ht.astype(BF16), preferred_element_type=F32)
        parts = []
        for r in range(SSM_REP):
            h = g * SSM_REP + r
            seg = cum_x2[:, r * LANES:(r + 1) * LANES] - cum_t[h:h + 1, :]
            w = cb * jnp.exp(jnp.where(causal, seg, -jnp.inf)) * dt_t[h:h + 1, :]
            xh = xs[:, h * SSM_HEAD_DIM:(h + 1) * SSM_HEAD_DIM].astype(BF16)
            parts.append(jnp.dot(w.astype(BF16), xh, preferred_element_type=F32))
        xg = xs[:, sl]
        yg = jnp.concatenate(parts, axis=1) + y_inter * jnp.exp(cum_x) + dx_ref[:, sl] * xg
        last = cum_x[L - 1:L, :]
        xw = (xg * (jnp.exp(last - cum_x) * dt_x)).astype(BF16)
        ht_ref[g] = ht * jnp.exp(last) + jnp.dot(bg.T.astype(BF16), xw, preferred_element_type=F32)
        zg = z[:, sl]
        yg = yg * (zg * jax.nn.sigmoid(zg))
        yg = yg * lax.rsqrt(jnp.mean(yg * yg, axis=-1, keepdims=True) + EPS) * gs_ref[:, sl]
        outs.append(yg[:rows_in].astype(y_ref.dtype))
    y_ref[...] = jnp.concatenate(outs, axis=1)

    @pl.when(c == nc - 1)
    def _():
        for g in range(SSM_GROUPS):
            hf_ref[0, g * SSM_REP:(g + 1) * SSM_REP] = ht_ref[g].T.reshape(SSM_REP, SSM_HEAD_DIM, SSM_STATE)


def _ssd(xbc, z, dt, conv_prev, h0, w_conv, b_conv, dt_bias, a_log, d_skip, g_ssm, *, n_seq, nc, rows_in, n_valid):
    L = SSM_CHUNK
    head_of = jnp.arange(SSM_HEADS * LANES, dtype=jnp.int32) // LANES
    e2 = (jnp.arange(LANES, dtype=jnp.int32)[:, None] == head_of[None, :]).astype(BF16)
    chan_head = jnp.arange(SSM_INNER, dtype=jnp.int32) // SSM_HEAD_DIM
    e1 = (jnp.arange(LANES, dtype=jnp.int32)[:, None] == chan_head[None, :]).astype(BF16)
    tril = (jnp.arange(L)[:, None] >= jnp.arange(L)[None, :]).astype(BF16)
    pad_l = lambda v: jnp.pad(v.reshape(1, -1), ((0, 0), (0, LANES - v.shape[0])))
    dx = jnp.repeat(d_skip, SSM_HEAD_DIM).reshape(1, SSM_INNER)
    const = lambda shape: pl.BlockSpec(shape, lambda b, c: (0,) * len(shape))
    kern = functools.partial(_ssd_kernel, L=L, nc=nc, n_valid=n_valid, rows_in=rows_in)
    return pl.pallas_call(
        kern, grid=(n_seq, nc),
        in_specs=[pl.BlockSpec((rows_in, CONV_DIM), lambda b, c: (b * nc + c, 0)),
                  pl.BlockSpec((rows_in, SSM_INNER), lambda b, c: (b * nc + c, 0)),
                  pl.BlockSpec((rows_in, LANES), lambda b, c: (b * nc + c, 0)),
                  pl.BlockSpec((1, CONV_PAD, CONV_DIM), lambda b, c: (b, 0, 0)),
                  pl.BlockSpec((1, SSM_HEADS, SSM_HEAD_DIM, SSM_STATE), lambda b, c: (b, 0, 0, 0)),
                  const((CONV_W, CONV_DIM)), const((1, CONV_DIM)), const((1, LANES)), const((1, LANES)),
                  const((1, SSM_INNER)), const((1, SSM_INNER)),
                  const((LANES, SSM_HEADS * LANES)), const((LANES, SSM_INNER)), const((L, L))],
        out_specs=[pl.BlockSpec((rows_in, SSM_INNER), lambda b, c: (b * nc + c, 0)),
                   pl.BlockSpec((1, SSM_HEADS, SSM_HEAD_DIM, SSM_STATE), lambda b, c: (b, 0, 0, 0)),
                   pl.BlockSpec((1, CONV_PAD, CONV_DIM), lambda b, c: (b, 0, 0))],
        out_shape=[jax.ShapeDtypeStruct((n_seq * nc * rows_in, SSM_INNER), BF16),
                   jax.ShapeDtypeStruct((n_seq, SSM_HEADS, SSM_HEAD_DIM, SSM_STATE), F32),
                   jax.ShapeDtypeStruct((n_seq, CONV_PAD, CONV_DIM), F32)],
        scratch_shapes=[pltpu.VMEM((SSM_GROUPS, SSM_STATE, SSM_REP * SSM_HEAD_DIM), F32),
                        pltpu.VMEM((CONV_PAD + L, CONV_DIM), F32)],
        compiler_params=_params(2), name="ssd")(
            xbc, z, dt, conv_prev, h0, w_conv, b_conv.reshape(1, -1), pad_l(dt_bias), pad_l(a_log), dx,
            g_ssm.reshape(1, -1), e2, e1, tril)


def kernel(x_prompt, x_sample, cache_k, cache_v, state_ssm, state_conv, page_table, c_prompt, c_sample, w_ada, b_ada, g_norm1, w_in, q_norm_g, k_norm_g, w_conv, b_conv, dt_bias, a_log, d_skip, g_ssm, w_attn_o, w_ssm_o, w_out, g_norm2, w_up, w_down):
    n_batch, seq, d = x_prompt.shape
    n_dec, dec_seq, _ = x_sample.shape
    n_p, n_s = n_batch * seq, n_dec * dec_seq
    assert n_s == DEC_ROWS and d == D_MODEL and seq % 1024 == 0
    n_pool = cache_k.shape[0]
    past_len = page_table.shape[1] * PAGE_SIZE
    n_blk = seq // MOBA_BLOCK
    row3 = lambda v: v.reshape(n_batch, 1, D_MODEL)
    rep_s = lambda v: jnp.repeat(v, dec_seq, axis=0)

    c_all = jnp.concatenate([c_prompt, c_sample], axis=0)
    c_pad = jnp.pad(c_all, ((0, -c_all.shape[0] % 16), (0, 0))).astype(BF16)
    mod = _mm(c_pad, w_ada, n_cols=6 * D_MODEL, tm=c_pad.shape[0], tn=512, tk=D_MODEL, out_dtype=F32,
              epi=_epi_bias, extras=(b_ada.reshape(1, -1),),
              extra_specs=[pl.BlockSpec((1, 512), lambda j, i, k: (0, j))], name="ada_mod")
    mod = mod[:n_batch + n_dec].reshape(n_batch + n_dec, 6, D_MODEL)
    sh1, sc1, gt1, sh2, sc2, gt2 = (mod[:, i] for i in range(6))
    pm = lambda v: (row3(v[:n_batch]), rep_s(v[n_batch:]))

    x = jnp.concatenate([x_prompt.reshape(n_p, D_MODEL), x_sample.reshape(n_s, D_MODEL)], axis=0)

    h = _norm_mod(x, g_norm1.reshape(1, -1), pm(sc1)[0], pm(sh1)[0], pm(sc1)[1], pm(sh1)[1],
                  rows_per_batch=seq, name="norm_mod1")
    proj = functools.partial(_mm, h, tm=1024, tk=D_MODEL, epi=_epi_store, out_dtype=F32)
    q = proj(w_in, n_cols=Q_DIM, col_off=0, tn=512, name="in_q")
    kv = proj(w_in, n_cols=2 * KV_DIM, col_off=Q_DIM, tn=512, name="in_kv")
    z = proj(w_in, n_cols=SSM_INNER, col_off=Q_DIM + 2 * KV_DIM, tn=512, name="in_z")
    xbc = proj(w_in, n_cols=CONV_DIM, col_off=Q_DIM + 2 * KV_DIM + SSM_INNER, tn=512, name="in_xbc")
    dt_off = Q_DIM + 2 * KV_DIM + SSM_INNER + CONV_DIM
    w_dt = jnp.pad(w_in[:, dt_off:dt_off + SSM_HEADS], ((0, 0), (0, LANES - SSM_HEADS)))
    dt = proj(w_dt, n_cols=LANES, tn=LANES, name="in_dt")
    gates = _mm(h, w_in[:, dt_off + SSM_HEADS:], n_cols=2 * D_MODEL, tm=1024, tn=512, tk=D_MODEL,
                out_dtype=F32, epi=_epi_sigmoid, name="in_gates")

    cp, sp = _rope_tables(jnp.arange(seq, dtype=jnp.int32))
    k_p, k_aug, v_aug, kmean = _k_prep(kv, cp, sp, k_norm_g.reshape(1, -1), n_rows=n_p, n_blk=n_blk)
    q_aug = _q_prep(q, cp, sp, q_norm_g.reshape(1, -1), kmean.reshape(n_batch, n_blk, KV_DIM),
                    n_rows=n_p, n_blk=n_blk)
    attn_p = _attn_prompt(q_aug, k_aug.reshape(n_batch * seq, 2 * KV_DIM), v_aug, n_batch=n_batch, seq=seq)

    pos_s = past_len + (jnp.arange(n_s, dtype=jnp.int32) % dec_seq)
    cs, ss = _rope_tables(pos_s)
    sample_block = n_p // DEC_ROWS
    q_s = _norm_rope_rows(q, sample_block, cs, ss, q_norm_g.reshape(1, -1), n_heads=N_HEADS, name="q_sample")
    k_s = _norm_rope_rows(kv, sample_block, cs, ss, k_norm_g.reshape(1, -1), n_heads=KV_HEADS, name="k_sample")
    v_s = kv[n_p:, KV_DIM:]
    q_t = q_s.reshape(n_dec, dec_seq, KV_HEADS, KV_REP, HEAD_DIM).transpose(0, 2, 4, 1, 3)
    q_t = q_t.reshape(n_dec, KV_HEADS, HEAD_DIM, GROUP_ROWS)
    eye = jnp.eye(KV_HEADS, dtype=F32)
    qbd_f32 = (q_t[:, :, :, None, :] * eye[None, :, None, :, None]).reshape(n_dec, KV_DIM, LANES)
    qbd = (qbd_f32 * ATTN_SCALE).astype(BF16)
    own_pad = lambda v: jnp.pad(v.reshape(n_dec, dec_seq, KV_DIM), ((0, 0), (0, 8 - dec_seq), (0, 0)))
    o_s = _attn_sample(page_table, cache_k.reshape(n_pool, PAGE_SIZE, KV_DIM),
                       cache_v.reshape(n_pool, PAGE_SIZE, KV_DIM), qbd, qbd_f32, own_pad(k_s), own_pad(v_s),
                       dec_seq=dec_seq)
    attn_s = o_s.reshape(n_dec, KV_HEADS, dec_seq, KV_REP, HEAD_DIM).transpose(0, 2, 1, 3, 4)
    attn = jnp.concatenate([attn_p, attn_s.reshape(n_s, Q_DIM).astype(BF16)], axis=0)

    ssd = functools.partial(_ssd, w_conv=w_conv, b_conv=b_conv, dt_bias=dt_bias, a_log=a_log, d_skip=d_skip,
                            g_ssm=g_ssm)
    conv_pad = lambda v: jnp.pad(v, ((0, 0), (CONV_PAD - (CONV_W - 1), 0), (0, 0)))
    zeros_h = jnp.zeros((n_batch, SSM_HEADS, SSM_HEAD_DIM, SSM_STATE), F32)
    zeros_c = jnp.zeros((n_batch, CONV_W - 1, CONV_DIM), F32)
    y_p, ssm_p, cnew_p = ssd(xbc, z, dt, conv_pad(zeros_c), zeros_h, n_seq=n_batch, nc=seq // SSM_CHUNK,
                             rows_in=SSM_CHUNK, n_valid=SSM_CHUNK)
    pad8 = lambda v: jnp.pad(v[n_p:].reshape(n_dec, dec_seq, -1), ((0, 0), (0, 8 - dec_seq), (0, 0))).reshape(
        n_dec * 8, -1)
    y_s, ssm_s, cnew_s = ssd(pad8(xbc), pad8(z), pad8(dt), conv_pad(state_conv), state_ssm, n_seq=n_dec, nc=1,
                             rows_in=8, n_valid=dec_seq)
    y_s = y_s.reshape(n_dec, 8, SSM_INNER)[:, :dec_seq].reshape(n_s, SSM_INNER)
    y_ssm = jnp.concatenate([y_p, y_s], axis=0)

    mixed = _mix(attn, y_ssm, w_attn_o, w_ssm_o, gates, tm=512, tn=256)
    res = lambda tm, tn: _residual_specs(tm, tn, seq, n_batch)
    x1 = _mm(mixed, w_out, n_cols=D_MODEL, tm=1024, tn=512, tk=D_MODEL, out_dtype=F32, epi=_epi_residual,
             extras=(x,) + pm(gt1), extra_specs=res(1024, 512), name="out_proj")

    h2 = _norm_mod(x1, g_norm2.reshape(1, -1), pm(sc2)[0], pm(sh2)[0], pm(sc2)[1], pm(sh2)[1],
                   rows_per_batch=seq, name="norm_mod2")
    u = _mm(h2, w_up, n_cols=D_FF, tm=1024, tn=512, tk=D_MODEL, out_dtype=BF16, epi=_epi_relu2, name="ffn_up")
    y = _mm(u, w_down, n_cols=D_MODEL, tm=2048, tn=512, tk=2048, out_dtype=F32, epi=_epi_residual,
            extras=(x1,) + pm(gt2), extra_specs=res(2048, 512), name="ffn_down")

    kv4 = lambda v, b, t: v.reshape(b, t, KV_HEADS, HEAD_DIM)
    first = CONV_PAD - (CONV_W - 1)
    return (y[:n_p].reshape(n_batch, seq, D_MODEL), y[n_p:].reshape(n_dec, dec_seq, D_MODEL),
            kv4(k_p, n_batch, seq), kv4(kv[:n_p, KV_DIM:], n_batch, seq),
            ssm_p, cnew_p[:, first:],
            kv4(k_s, n_dec, dec_seq), kv4(v_s, n_dec, dec_seq),
            ssm_s, cnew_s[:, first:])
```

```python
import functools

import jax
import jax.numpy as jnp
from jax import lax
from jax.experimental import pallas as pl
from jax.experimental.pallas import tpu as pltpu

F32, BF16 = jnp.float32, jnp.bfloat16

D_MODEL = 4096
HEAD_DIM = 128
N_HEADS = 32
KV_HEADS = 8
KV_REP = 4
MOBA_BLOCK = 256
MOBA_TOPK = 3
ROPE_THETA = 10000.0
PAGE_SIZE = 128
SSM_HEADS = 64
SSM_HEAD_DIM = 64
SSM_STATE = 128
SSM_GROUPS = 8
SSM_REP = 8
SSM_INNER = 4096
SSM_CHUNK = 128
CONV_W = 4
CONV_DIM = 6144
D_FF = 16384
EPS = 1e-6
Q_DIM = N_HEADS * HEAD_DIM
KV_DIM = KV_HEADS * HEAD_DIM
ATTN_SCALE = HEAD_DIM ** -0.5
MASK_SCORE = -1e30
VMEM_LIMIT_BYTES = 60 * 1024 * 1024
LANES = 128
DEC_ROWS = 128
ROW_TILES = 5
DOWN_ROW_TILES = 4


def _params(n_axes):
    return pltpu.CompilerParams(dimension_semantics=("arbitrary",) * n_axes,
                                vmem_limit_bytes=VMEM_LIMIT_BYTES)


def _split3(x):
    hi = x.astype(BF16)
    r1 = x - hi.astype(F32)
    mid = r1.astype(BF16)
    lo = (r1 - mid.astype(F32)).astype(BF16)
    return hi, mid, lo


def _dot3_lhs_exact(w_exact, x):
    hi, mid, lo = _split3(x)
    f = functools.partial(jnp.dot, preferred_element_type=F32)
    return f(w_exact, lo) + f(w_exact, mid) + f(w_exact, hi)


def _dot3_rhs_exact(x, w_exact):
    hi, mid, lo = _split3(x)
    f = functools.partial(jnp.dot, preferred_element_type=F32)
    return f(lo, w_exact) + f(mid, w_exact) + f(hi, w_exact)


def _dot_hp(a, b, dims):
    a0, a1, a2 = _split3(a)
    b0, b1, b2 = _split3(b)
    f = functools.partial(lax.dot_general, dimension_numbers=dims, preferred_element_type=F32)
    return (f(a2, b0) + f(a1, b1) + f(a0, b2)) + (f(a1, b0) + f(a0, b1)) + f(a0, b0)


def _norm_rope(xh, g, c, s):
    xn = xh * lax.rsqrt(jnp.mean(xh * xh, axis=-1, keepdims=True) + EPS) * g
    return xn * c + pltpu.roll(xn, HEAD_DIM // 2, 1) * s


def _topk_select(gate, n_idx, n_blocks):
    cnt = jnp.zeros(gate.shape, F32)
    for m in range(n_blocks):
        rm = gate[m:m + 1, :]
        beats = jnp.where(rm > gate, 1.0, jnp.where((rm == gate) & (n_idx > m), 1.0, 0.0))
        cnt = cnt + beats
    return cnt < MOBA_TOPK


def _rope_tables(pos):
    half = HEAD_DIM // 2
    inv = ROPE_THETA ** (-jnp.arange(half, dtype=F32) / half)
    ang = pos.astype(F32)[:, None] * inv[None, :]
    c, s = jnp.cos(ang), jnp.sin(ang)
    return jnp.concatenate([c, c], axis=-1), jnp.concatenate([-s, s], axis=-1)


def _row_segments(tm, n_tiles, seq, n_batch, n_s):
    n_p = seq * n_batch
    tiles = []
    for t in range(n_tiles):
        lo, hi, cur = t * tm, (t + 1) * tm, []
        for b in range(n_batch):
            a, e = max(lo, b * seq), min(hi, (b + 1) * seq)
            if a < e:
                cur.append((a - lo, e - lo, "p", b))
        a, e = max(lo, n_p), min(hi, n_p + n_s)
        if a < e:
            assert (a, e) == (n_p, n_p + n_s) and (a - lo) % 8 == 0
            cur.append((a - lo, e - lo, "s", 0))
        tiles.append(cur)
    return tiles


def _per_tile(i, tiles, fn):
    for t, segs in enumerate(tiles):
        pl.when(i == t)(functools.partial(fn, segs))


def _epi_store(acc, i, ex, outs):
    outs[0][...] = acc.astype(outs[0].dtype)


def _epi_bias(acc, i, ex, outs):
    outs[0][...] = acc + ex[0][...]


def _epi_sigmoid(acc, i, ex, outs):
    outs[0][...] = jax.nn.sigmoid(acc)


def _epi_relu2(acc, i, ex, outs):
    r = jnp.maximum(acc, 0.0)
    outs[0][...] = (r * r).astype(outs[0].dtype)


def _epi_gate(acc, i, ex, outs):
    outs[0][...] = ex[0][...] * acc


def _epi_gate_add(acc, i, ex, outs):
    outs[0][...] = (ex[0][...] + ex[1][...] * acc).astype(outs[0].dtype)


def _make_epi_residual(tiles):
    def epi(acc, i, ex, outs):
        xp_ref, xs_ref, gp_ref, gs_ref = ex
        o_ref, = outs

        def write(segs):
            for r0, r1, kind, b in segs:
                if kind == "p":
                    o_ref[r0:r1, :] = xp_ref[r0:r1, :] + gp_ref[b] * acc[r0:r1]
                else:
                    o_ref[r0:r1, :] = xs_ref[...] + gs_ref[...] * acc[r0:r1]

        _per_tile(i, tiles, write)

    return epi


def _mm_kernel(a_ref, w_ref, *refs, n_ex, epi, shift, w_is_t):
    if shift:
        w = jnp.concatenate([w_ref[shift:, :], refs[0][...]], axis=0)
        refs = refs[1:]
    else:
        w = w_ref[...]
    dims = (((1,), (1,)), ((), ())) if w_is_t else (((1,), (0,)), ((), ()))
    acc = lax.dot_general(a_ref[...], w.astype(BF16), dims, preferred_element_type=F32)
    epi(acc, pl.program_id(0), refs[:n_ex], refs[n_ex:])


def _mm(a, w, *, n_cols, col_off=0, shift=0, w_is_t=False, n_i, tn, out_dtype, epi, extras=(), extra_specs=(),
        name):
    m, kdim = a.shape
    tm = m // n_i
    assert tm * n_i == m and col_off % tn == 0 and n_cols % tn == 0 and (not shift or w_is_t)
    off = col_off // tn
    if w_is_t:
        w_spec = pl.BlockSpec((tn, kdim), lambda i, j: (j + off, 0))
    else:
        w_spec = pl.BlockSpec((kdim, tn), lambda i, j: (0, j + off))
    in_specs = [pl.BlockSpec((tm, kdim), lambda i, j: (i, 0)), w_spec]
    args = [a, w]
    if shift:
        assert shift % 8 == 0 and tn % shift == 0
        per = tn // shift
        in_specs.append(pl.BlockSpec((shift, kdim), lambda i, j: ((j + off + 1) * per, 0)))
        args.append(w)
    kern = functools.partial(_mm_kernel, n_ex=len(extras), epi=epi, shift=shift, w_is_t=w_is_t)
    return pl.pallas_call(
        kern, grid=(n_i, n_cols // tn), in_specs=in_specs + list(extra_specs),
        out_specs=pl.BlockSpec((tm, tn), lambda i, j: (i, j)),
        out_shape=jax.ShapeDtypeStruct((m, n_cols), out_dtype),
        compiler_params=_params(2), name=name)(*args, *extras)


def _ffn_down_kernel(a_ref, w_ref, x_ref, gp_ref, gs_ref, yp_ref, ys_ref, *, nk, tiles):
    i, k = pl.program_id(0), pl.program_id(2)
    @pl.when(k == 0)
    def _():
        yp_ref[...] = jnp.zeros(yp_ref.shape, F32)

    yp_ref[...] += jnp.dot(a_ref[...], w_ref[...].astype(BF16), preferred_element_type=F32)

    @pl.when(k == nk - 1)
    def _():
        def write(segs):
            for r0, r1, kind, b in segs:
                if kind == "s":
                    ys_ref[...] = x_ref[r0:r1, :] + gs_ref[...] * yp_ref[r0:r1, :]
            for r0, r1, kind, b in segs:
                if kind == "p":
                    yp_ref[r0:r1, :] = x_ref[r0:r1, :] + gp_ref[b] * yp_ref[r0:r1, :]

        _per_tile(i, tiles, write)


def _ffn_down(u, w, x, gate_p, gate_s, *, seq, n_batch, tn, tk):
    m, kdim = u.shape
    n_i = DOWN_ROW_TILES
    tm, nk, n_j = m // n_i, kdim // tk, D_MODEL // tn
    n_p = seq * n_batch
    tiles = _row_segments(tm, n_i, seq, n_batch, m - n_p)
    kern = functools.partial(_ffn_down_kernel, nk=nk, tiles=tiles)
    return pl.pallas_call(
        kern, grid=(n_i, n_j, nk),
        in_specs=[pl.BlockSpec((tm, tk), lambda i, j, k: (i, k)),
                  pl.BlockSpec((tk, tn), lambda i, j, k: (k, j)),
                  pl.BlockSpec((tm, tn), lambda i, j, k: (i, j)),
                  pl.BlockSpec((n_batch, 1, tn), lambda i, j, k: (0, 0, j)),
                  pl.BlockSpec((DEC_ROWS, tn), lambda i, j, k: (0, j))],
        out_specs=[pl.BlockSpec((tm, tn), lambda i, j, k: (i, j)),
                   pl.BlockSpec((DEC_ROWS, tn), lambda i, j, k: (0, jnp.where(i == n_i - 1, j, 0)))],
        out_shape=[jax.ShapeDtypeStruct((n_p, D_MODEL), F32), jax.ShapeDtypeStruct((m - n_p, D_MODEL), F32)],
        compiler_params=_params(3), name="ffn_down")(u, w, x, gate_p, gate_s)


def _place_kernel(big_ref, small_ref, o_ref):
    o_ref[...] = small_ref[...]


def _place_rows(big, small, row_block, name):
    return pl.pallas_call(
        _place_kernel, grid=(1,),
        in_specs=[pl.BlockSpec(memory_space=pl.ANY), pl.BlockSpec(small.shape, lambda i: (0, 0))],
        out_specs=pl.BlockSpec(small.shape, lambda i: (row_block, 0)),
        out_shape=jax.ShapeDtypeStruct(big.shape, big.dtype),
        input_output_aliases={0: 0}, compiler_params=_params(1), name=name)(big, small)


def _norm_mod_kernel(xp_ref, xs_ref, g_ref, scp_ref, shp_ref, scs_ref, shs_ref, o_ref, *, n_i):
    is_sample = pl.program_id(0) == n_i - 1
    x = jnp.where(is_sample, xs_ref[...], xp_ref[...])
    y = x * lax.rsqrt(jnp.mean(x * x, axis=-1, keepdims=True) + EPS) * g_ref[...]
    sc = jnp.where(is_sample, scs_ref[...], scp_ref[0])
    sh = jnp.where(is_sample, shs_ref[...], shp_ref[0])
    o_ref[...] = (y * (1.0 + sc) + sh).astype(o_ref.dtype)


def _norm_mod(x_p, x_s, s_block, g, sc_p, sh_p, sc_s, sh_s, *, n_p, rows_per_batch, name):
    d = x_p.shape[1]
    tm = DEC_ROWS
    n_i = n_p // tm + 1
    tpb = rows_per_batch // tm
    nb = sc_p.shape[0]
    pspec = pl.BlockSpec((1, 1, d), lambda i: (jnp.minimum(i // tpb, nb - 1), 0, 0))
    sspec = pl.BlockSpec((tm, d), lambda i: (0, 0))
    return pl.pallas_call(
        functools.partial(_norm_mod_kernel, n_i=n_i), grid=(n_i,),
        in_specs=[pl.BlockSpec((tm, d), lambda i: (jnp.minimum(i, n_i - 2), 0)),
                  pl.BlockSpec((tm, d), lambda i: (s_block, 0)),
                  pl.BlockSpec((1, d), lambda i: (0, 0)), pspec, pspec, sspec, sspec],
        out_specs=pl.BlockSpec((tm, d), lambda i: (i, 0)),
        out_shape=jax.ShapeDtypeStruct((n_i * tm, d), BF16),
        compiler_params=_params(1), name=name)(x_p, x_s, g, sc_p, sh_p, sc_s, sh_s)


def _norm_rope_kernel(x_ref, c_ref, s_ref, g_ref, o_ref, *, n_heads):
    c, s, g = c_ref[...], s_ref[...], g_ref[...]
    for h in range(n_heads):
        sl = slice(h * HEAD_DIM, (h + 1) * HEAD_DIM)
        o_ref[:, sl] = _norm_rope(x_ref[:, sl], g, c, s)


def _norm_rope_rows(x, row_block, c, s, g, *, n_heads, name):
    w = n_heads * HEAD_DIM
    return pl.pallas_call(
        functools.partial(_norm_rope_kernel, n_heads=n_heads), grid=(1,),
        in_specs=[pl.BlockSpec((DEC_ROWS, w), lambda i: (row_block, 0)),
                  pl.BlockSpec((DEC_ROWS, HEAD_DIM), lambda i: (0, 0)),
                  pl.BlockSpec((DEC_ROWS, HEAD_DIM), lambda i: (0, 0)),
                  pl.BlockSpec((1, HEAD_DIM), lambda i: (0, 0))],
        out_specs=pl.BlockSpec((DEC_ROWS, w), lambda i: (0, 0)),
        out_shape=jax.ShapeDtypeStruct((DEC_ROWS, w), F32),
        compiler_params=_params(1), name=name)(x, c, s, g)


def _k_prep_kernel(k_ref, v_ref, c_ref, s_ref, g_ref, ko_ref, ka_ref, va_ref, km_ref, *, n_blk):
    blk = pl.program_id(0) % n_blk
    c, s, g = c_ref[...], s_ref[...], g_ref[...]
    lane = lax.broadcasted_iota(jnp.int32, (MOBA_BLOCK, HEAD_DIM), 1)
    block_tag = jnp.where(lane == blk, 1.0, 0.0).astype(BF16)
    ones_col = jnp.where(lane == 0, 1.0, 0.0).astype(BF16)
    for h in range(KV_HEADS):
        sl = slice(h * HEAD_DIM, (h + 1) * HEAD_DIM)
        lo, mid, hi = 2 * h * HEAD_DIM, (2 * h + 1) * HEAD_DIM, (2 * h + 2) * HEAD_DIM
        kr = _norm_rope(k_ref[:, sl], g, c, s)
        ko_ref[:, sl] = kr
        ka_ref[:, lo:mid] = kr.astype(BF16)
        ka_ref[:, mid:hi] = block_tag
        va_ref[:, lo:mid] = v_ref[:, sl].astype(BF16)
        va_ref[:, mid:hi] = ones_col
        km_ref[0, :, sl] = jnp.mean(kr, axis=0, keepdims=True)


def _k_prep(kv, c, s, g, *, n_rows, n_blk):
    nt = n_rows // MOBA_BLOCK
    return pl.pallas_call(
        functools.partial(_k_prep_kernel, n_blk=n_blk), grid=(nt,),
        in_specs=[pl.BlockSpec((MOBA_BLOCK, KV_DIM), lambda t: (t, 0)),
                  pl.BlockSpec((MOBA_BLOCK, KV_DIM), lambda t: (t, 1)),
                  pl.BlockSpec((MOBA_BLOCK, HEAD_DIM), lambda t: (t % n_blk, 0)),
                  pl.BlockSpec((MOBA_BLOCK, HEAD_DIM), lambda t: (t % n_blk, 0)),
                  pl.BlockSpec((1, HEAD_DIM), lambda t: (0, 0))],
        out_specs=[pl.BlockSpec((MOBA_BLOCK, KV_DIM), lambda t: (t, 0)),
                   pl.BlockSpec((MOBA_BLOCK, 2 * KV_DIM), lambda t: (t, 0)),
                   pl.BlockSpec((MOBA_BLOCK, 2 * KV_DIM), lambda t: (t, 0)),
                   pl.BlockSpec((1, 1, KV_DIM), lambda t: (t, 0, 0))],
        out_shape=[jax.ShapeDtypeStruct((n_rows, KV_DIM), F32),
                   jax.ShapeDtypeStruct((n_rows, 2 * KV_DIM), BF16),
                   jax.ShapeDtypeStruct((n_rows, 2 * KV_DIM), BF16),
                   jax.ShapeDtypeStruct((nt, 1, KV_DIM), F32)],
        compiler_params=_params(1), name="k_prep")(kv, kv, c, s, g)


def _q_prep_kernel(q_ref, c_ref, s_ref, g_ref, km_ref, o_ref, *, n_blk):
    own = pl.program_id(0) % n_blk
    c, s, g = c_ref[...], s_ref[...], g_ref[...]
    n_idx = lax.broadcasted_iota(jnp.int32, (n_blk, MOBA_BLOCK), 0)
    lane = lax.broadcasted_iota(jnp.int32, (MOBA_BLOCK, HEAD_DIM), 1)
    past = n_idx < own
    km = km_ref[0]
    pad = jnp.zeros((HEAD_DIM - n_blk, MOBA_BLOCK), F32)
    for h in range(N_HEADS):
        kv = h // KV_REP
        qh = _norm_rope(q_ref[:, h * HEAD_DIM:(h + 1) * HEAD_DIM], g, c, s)
        gate = _dot_hp(km[:, kv * HEAD_DIM:(kv + 1) * HEAD_DIM], qh, (((1,), (1,)), ((), ())))
        gate = jnp.where(past, gate, -jnp.inf)
        sel = (_topk_select(gate, n_idx, n_blk) & past) | (n_idx == own)
        sel_t = jnp.concatenate([jnp.where(sel, 1.0, 0.0), pad], axis=0).T
        bias = jnp.where(lane < n_blk, (sel_t - 1.0) * (-MASK_SCORE), 0.0)
        o_ref[:, 2 * h * HEAD_DIM:(2 * h + 1) * HEAD_DIM] = (qh * ATTN_SCALE).astype(BF16)
        o_ref[:, (2 * h + 1) * HEAD_DIM:(2 * h + 2) * HEAD_DIM] = bias.astype(BF16)


def _q_prep(q, c, s, g, kmean, *, n_rows, n_blk):
    nt = n_rows // MOBA_BLOCK
    return pl.pallas_call(
        functools.partial(_q_prep_kernel, n_blk=n_blk), grid=(nt,),
        in_specs=[pl.BlockSpec((MOBA_BLOCK, Q_DIM), lambda t: (t, 0)),
                  pl.BlockSpec((MOBA_BLOCK, HEAD_DIM), lambda t: (t % n_blk, 0)),
                  pl.BlockSpec((MOBA_BLOCK, HEAD_DIM), lambda t: (t % n_blk, 0)),
                  pl.BlockSpec((1, HEAD_DIM), lambda t: (0, 0)),
                  pl.BlockSpec((1, n_blk, KV_DIM), lambda t: (t // n_blk, 0, 0))],
        out_specs=pl.BlockSpec((MOBA_BLOCK, 2 * Q_DIM), lambda t: (t, 0)),
        out_shape=jax.ShapeDtypeStruct((n_rows, 2 * Q_DIM), BF16),
        compiler_params=_params(1), name="q_prep")(q, c, s, g, kmean)


KEY_GROUP = 4 * MOBA_BLOCK


def _attn_p_kernel(q_ref, k_ref, v_ref, o_ref):
    i = pl.program_id(2)
    rows = KV_REP * MOBA_BLOCK
    aug = 2 * HEAD_DIM
    qs = jnp.concatenate([q_ref[:, r * aug:(r + 1) * aug] for r in range(KV_REP)], axis=0)
    dims = (((1,), (1,)), ((), ()))
    n_full = (i * MOBA_BLOCK) // KEY_GROUP

    def scores(gi):
        st = pl.multiple_of(gi * KEY_GROUP, KEY_GROUP)
        s = lax.dot_general(qs, k_ref[pl.ds(st, KEY_GROUP), :], dims, preferred_element_type=F32)
        return s, v_ref[pl.ds(st, KEY_GROUP), :]

    s, vg = scores(n_full)
    qpos = i * MOBA_BLOCK + (lax.broadcasted_iota(jnp.int32, (rows, KEY_GROUP), 0) & (MOBA_BLOCK - 1))
    kpos = n_full * KEY_GROUP + lax.broadcasted_iota(jnp.int32, (rows, KEY_GROUP), 1)
    s = jnp.where(kpos <= qpos, s, -jnp.inf)
    m = jnp.max(s, axis=1, keepdims=True)
    acc = jnp.dot(jnp.exp(s - m).astype(BF16), vg, preferred_element_type=F32)

    def body(gi, carry):
        m, acc = carry
        s, vg = scores(gi)
        m_new = jnp.maximum(m, jnp.max(s, axis=1, keepdims=True))
        acc = acc * jnp.exp(m - m_new) + jnp.dot(jnp.exp(s - m_new).astype(BF16), vg, preferred_element_type=F32)
        return m_new, acc

    m, acc = lax.fori_loop(0, n_full, body, (m, acc))
    o = acc[:, :HEAD_DIM] / acc[:, HEAD_DIM:HEAD_DIM + 1]
    for r in range(KV_REP):
        o_ref[:, r * HEAD_DIM:(r + 1) * HEAD_DIM] = o[r * MOBA_BLOCK:(r + 1) * MOBA_BLOCK].astype(o_ref.dtype)


def _attn_prompt(q_aug, k_aug, v_aug, *, n_batch, seq, out_rows):
    n_blk = seq // MOBA_BLOCK
    aug = 2 * HEAD_DIM
    return pl.pallas_call(
        _attn_p_kernel, grid=(n_batch, KV_HEADS, n_blk),
        in_specs=[pl.BlockSpec((MOBA_BLOCK, KV_REP * aug), lambda b, g, i: (b * n_blk + i, g)),
                  pl.BlockSpec((seq, aug), lambda b, g, i: (b, g)),
                  pl.BlockSpec((seq, aug), lambda b, g, i: (b, g))],
        out_specs=pl.BlockSpec((MOBA_BLOCK, KV_REP * HEAD_DIM), lambda b, g, i: (b * n_blk + i, g)),
        out_shape=jax.ShapeDtypeStruct((out_rows, Q_DIM), BF16),
        compiler_params=_params(3), name="attn_prompt")(q_aug, k_aug, v_aug)


GROUP_ROWS = 16


def _diag_blocks(full):
    return jnp.concatenate([full[g * GROUP_ROWS:(g + 1) * GROUP_ROWS, g * HEAD_DIM:(g + 1) * HEAD_DIM]
                            for g in range(KV_HEADS)], axis=0)


def _page_tile(ref):
    return jnp.concatenate([ref[0, pl.ds(g, PAGE_SIZE, stride=KV_HEADS), :] for g in range(KV_HEADS)], axis=1)


def _attn_s_kernel(pt_ref, k0_ref, k1_ref, v0_ref, v1_ref, qb_ref, qf_ref, ko_ref, vo_ref, o_ref,
                   ob_ref, m_ref, l_ref, ks_ref, *, n_blk, dec_seq):
    jb = pl.program_id(1)
    kblk = jnp.concatenate([_page_tile(k0_ref), _page_tile(k1_ref)], axis=0)
    vblk = jnp.concatenate([_page_tile(v0_ref), _page_tile(v1_ref)], axis=0).astype(BF16)
    qb = qb_ref[0]
    ks_ref[pl.ds(jb, 1), :] = jnp.sum(kblk, axis=0, keepdims=True)
    st = jnp.dot(kblk.astype(BF16), qb, preferred_element_type=F32)
    mb = jnp.max(st, axis=0, keepdims=True)
    p = jnp.exp(st - mb)
    m_ref[pl.ds(jb, 1), :] = mb
    l_ref[pl.ds(jb, 1), :] = jnp.sum(p, axis=0, keepdims=True)
    ob_ref[jb] = _diag_blocks(jnp.dot(p.T.astype(BF16), vblk, preferred_element_type=F32))

    @pl.when(jb == n_blk - 1)
    def _():
        gate = _dot_hp(ks_ref[...] * (1.0 / MOBA_BLOCK), qf_ref[0], (((1,), (0,)), ((), ())))
        n_idx = lax.broadcasted_iota(jnp.int32, (n_blk, LANES), 0)
        sel = _topk_select(gate, n_idx, n_blk)
        m_all = m_ref[...]
        so = jnp.dot(ko_ref[0].astype(BF16), qb, preferred_element_type=F32)
        krow = lax.broadcasted_iota(jnp.int32, so.shape, 0)
        tq = (lax.broadcasted_iota(jnp.int32, so.shape, 1) >> 2) & (dec_seq - 1)
        so = jnp.where((krow <= tq) & (krow < dec_seq), so, -jnp.inf)
        m_tot = jnp.maximum(jnp.max(jnp.where(sel, m_all, -jnp.inf), axis=0, keepdims=True),
                            jnp.max(so, axis=0, keepdims=True))
        w = jnp.where(sel, jnp.exp(m_all - m_tot), 0.0)
        po = jnp.exp(so - m_tot)
        inv = 1.0 / (jnp.sum(w * l_ref[...], axis=0, keepdims=True) + jnp.sum(po, axis=0, keepdims=True))
        pad = jnp.zeros((LANES - n_blk - so.shape[0], LANES), F32)
        wt = jnp.concatenate([w * inv, po * inv, pad], axis=0).T
        acc = jnp.zeros((LANES, HEAD_DIM), F32)
        for b in range(n_blk):
            acc = acc + wt[:, b:b + 1] * ob_ref[b]
        vo = vo_ref[0]
        for t in range(dec_seq):
            vrow = jnp.concatenate(
                [jnp.broadcast_to(vo[t:t + 1, g * HEAD_DIM:(g + 1) * HEAD_DIM], (GROUP_ROWS, HEAD_DIM))
                 for g in range(KV_HEADS)], axis=0)
            acc = acc + wt[:, n_blk + t:n_blk + t + 1] * vrow
        o_ref[0] = acc


def _attn_sample(page_table, cache_k, cache_v, qbd, qbd_f32, k_own, v_own, *, dec_seq):
    n_seq, n_pages = page_table.shape
    n_blk = n_pages * PAGE_SIZE // MOBA_BLOCK
    page_rows = PAGE_SIZE * KV_HEADS
    page = pl.BlockSpec((1, page_rows, HEAD_DIM), lambda s, jb, pt: (pt[s, 2 * jb], 0, 0))
    page1 = pl.BlockSpec((1, page_rows, HEAD_DIM), lambda s, jb, pt: (pt[s, 2 * jb + 1], 0, 0))
    per_seq = lambda r, c: pl.BlockSpec((1, r, c), lambda s, jb, pt: (s, 0, 0))
    grid_spec = pltpu.PrefetchScalarGridSpec(
        num_scalar_prefetch=1, grid=(n_seq, n_blk),
        in_specs=[page, page1, page, page1, per_seq(KV_DIM, LANES), per_seq(KV_DIM, LANES),
                  per_seq(8, KV_DIM), per_seq(8, KV_DIM)],
        out_specs=per_seq(LANES, HEAD_DIM),
        scratch_shapes=[pltpu.VMEM((n_blk, LANES, HEAD_DIM), F32), pltpu.VMEM((n_blk, LANES), F32),
                        pltpu.VMEM((n_blk, LANES), F32), pltpu.VMEM((n_blk, KV_DIM), F32)])
    return pl.pallas_call(
        functools.partial(_attn_s_kernel, n_blk=n_blk, dec_seq=dec_seq), grid_spec=grid_spec,
        out_shape=jax.ShapeDtypeStruct((n_seq, LANES, HEAD_DIM), F32),
        compiler_params=_params(2), name="attn_sample")(
            page_table, cache_k, cache_k, cache_v, cache_v, qbd, qbd_f32, k_own, v_own)


CONV_PAD = 8


def _ssd_kernel(xbc_ref, z_ref, dt_ref, cprev_ref, h0_ref, wc_ref, bc_ref, dtb_ref, alog_ref, dx_ref, gs_ref,
                e2_ref, e1_ref, tril_ref, y_ref, hf_ref, cnew_ref, ht_ref, cbuf_ref, *, L, nc, n_valid, rows_in):
    c = pl.program_id(1)
    gw = SSM_REP * SSM_HEAD_DIM

    @pl.when(c == 0)
    def _():
        cbuf_ref[0:CONV_PAD, :] = cprev_ref[0]
        if rows_in < L:
            cbuf_ref[CONV_PAD + rows_in:, :] = jnp.zeros((L - rows_in, CONV_DIM), F32)
        for g in range(SSM_GROUPS):
            ht_ref[g] = h0_ref[0, g * SSM_REP:(g + 1) * SSM_REP].reshape(gw, SSM_STATE).T

    cbuf_ref[CONV_PAD:CONV_PAD + rows_in, :] = xbc_ref[...]
    first = CONV_PAD - (CONV_W - 1)
    conv = bc_ref[...]
    for j in range(CONV_W):
        conv = conv + cbuf_ref[first + j:first + j + L, :] * wc_ref[j:j + 1, :]
    cnew_ref[0] = cbuf_ref[n_valid:CONV_PAD + n_valid, :]
    cbuf_ref[first:CONV_PAD, :] = cbuf_ref[first + L:CONV_PAD + L, :]

    act = conv * jax.nn.sigmoid(conv)
    xs = act[:, :SSM_INNER]
    if rows_in < L:
        zpad = jnp.zeros((L - rows_in, SSM_INNER), F32)
        z = jnp.concatenate([z_ref[...], zpad], axis=0)
        dt_raw = jnp.concatenate([dt_ref[...], jnp.zeros((L - rows_in, LANES), F32)], axis=0)
    else:
        z, dt_raw = z_ref[...], dt_ref[...]

    row = lax.broadcasted_iota(jnp.int32, (L, LANES), 0)
    lane = lax.broadcasted_iota(jnp.int32, (L, LANES), 1)
    dt = jax.nn.softplus(dt_raw + dtb_ref[...])
    dt = jnp.where((row < n_valid) & (lane < SSM_HEADS), dt, 0.0)
    cum = _dot3_lhs_exact(tril_ref[...], dt * (-jnp.exp(alog_ref[...])))
    cum_t, dt_t = cum.T, dt.T
    causal = lax.broadcasted_iota(jnp.int32, (L, L), 1) <= lax.broadcasted_iota(jnp.int32, (L, L), 0)

    outs = []
    for g in range(SSM_GROUPS):
        sl = slice(g * gw, (g + 1) * gw)
        bg = act[:, SSM_INNER + g * SSM_STATE:SSM_INNER + (g + 1) * SSM_STATE]
        cg = act[:, SSM_INNER + (SSM_GROUPS + g) * SSM_STATE:SSM_INNER + (SSM_GROUPS + g + 1) * SSM_STATE]
        bg_b, cg_b = bg.astype(BF16), cg.astype(BF16)
        cb = lax.dot_general(cg_b, bg_b, (((1,), (1,)), ((), ())), preferred_element_type=F32)
        cum_x2 = _dot3_rhs_exact(cum, e2_ref[:, g * SSM_REP * LANES:(g + 1) * SSM_REP * LANES])
        cum_x = _dot3_rhs_exact(cum, e1_ref[:, sl])
        dt_x = _dot3_rhs_exact(dt, e1_ref[:, sl])
        ht = ht_ref[g]
        y_inter = jnp.dot(cg_b, ht.astype(BF16), preferred_element_type=F32)
        parts = []
        for r in range(SSM_REP):
            h = g * SSM_REP + r
            seg = cum_x2[:, r * LANES:(r + 1) * LANES] - cum_t[h:h + 1, :]
            w = cb * jnp.exp(jnp.where(causal, seg, -jnp.inf)) * dt_t[h:h + 1, :]
            xh = xs[:, h * SSM_HEAD_DIM:(h + 1) * SSM_HEAD_DIM].astype(BF16)
            parts.append(jnp.dot(w.astype(BF16), xh, preferred_element_type=F32))
        xg = xs[:, sl]
        yg = jnp.concatenate(parts, axis=1) + y_inter * jnp.exp(cum_x) + dx_ref[:, sl] * xg
        last = cum_x[L - 1:L, :]
        xw = (xg * (jnp.exp(last - cum_x) * dt_x)).astype(BF16)
        ht_ref[g] = ht * jnp.exp(last) + jnp.dot(bg.T.astype(BF16), xw, preferred_element_type=F32)
        zg = z[:, sl]
        yg = yg * (zg * jax.nn.sigmoid(zg))
        yg = yg * lax.rsqrt(jnp.mean(yg * yg, axis=-1, keepdims=True) + EPS) * gs_ref[:, sl]
        outs.append(yg[:rows_in].astype(y_ref.dtype))
    y_ref[...] = jnp.concatenate(outs, axis=1)

    @pl.when(c == nc - 1)
    def _():
        for g in range(SSM_GROUPS):
            hf_ref[0, g * SSM_REP:(g + 1) * SSM_REP] = ht_ref[g].T.reshape(SSM_REP, SSM_HEAD_DIM, SSM_STATE)


def _ssd(xbc, z, dt, conv_prev, h0, w_conv, b_conv, dt_bias, a_log, d_skip, g_ssm, *, n_seq, nc, rows_in, n_valid,
         out_rows):
    L = SSM_CHUNK
    head_of = jnp.arange(SSM_HEADS * LANES, dtype=jnp.int32) // LANES
    e2 = (jnp.arange(LANES, dtype=jnp.int32)[:, None] == head_of[None, :]).astype(BF16)
    chan_head = jnp.arange(SSM_INNER, dtype=jnp.int32) // SSM_HEAD_DIM
    e1 = (jnp.arange(LANES, dtype=jnp.int32)[:, None] == chan_head[None, :]).astype(BF16)
    tril = (jnp.arange(L)[:, None] >= jnp.arange(L)[None, :]).astype(BF16)
    pad_l = lambda v: jnp.pad(v.reshape(1, -1), ((0, 0), (0, LANES - v.shape[0])))
    dx = jnp.repeat(d_skip, SSM_HEAD_DIM).reshape(1, SSM_INNER)
    const = lambda shape: pl.BlockSpec(shape, lambda b, c: (0,) * len(shape))
    kern = functools.partial(_ssd_kernel, L=L, nc=nc, n_valid=n_valid, rows_in=rows_in)
    return pl.pallas_call(
        kern, grid=(n_seq, nc),
        in_specs=[pl.BlockSpec((rows_in, CONV_DIM), lambda b, c: (b * nc + c, 0)),
                  pl.BlockSpec((rows_in, SSM_INNER), lambda b, c: (b * nc + c, 0)),
                  pl.BlockSpec((rows_in, LANES), lambda b, c: (b * nc + c, 0)),
                  pl.BlockSpec((1, CONV_PAD, CONV_DIM), lambda b, c: (b, 0, 0)),
                  pl.BlockSpec((1, SSM_HEADS, SSM_HEAD_DIM, SSM_STATE), lambda b, c: (b, 0, 0, 0)),
                  const((CONV_W, CONV_DIM)), const((1, CONV_DIM)), const((1, LANES)), const((1, LANES)),
                  const((1, SSM_INNER)), const((1, SSM_INNER)),
                  const((LANES, SSM_HEADS * LANES)), const((LANES, SSM_INNER)), const((L, L))],
        out_specs=[pl.BlockSpec((rows_in, SSM_INNER), lambda b, c: (b * nc + c, 0)),
                   pl.BlockSpec((1, SSM_HEADS, SSM_HEAD_DIM, SSM_STATE), lambda b, c: (b, 0, 0, 0)),
                   pl.BlockSpec((1, CONV_PAD, CONV_DIM), lambda b, c: (b, 0, 0))],
        out_shape=[jax.ShapeDtypeStruct((out_rows, SSM_INNER), BF16),
                   jax.ShapeDtypeStruct((n_seq, SSM_HEADS, SSM_HEAD_DIM, SSM_STATE), F32),
                   jax.ShapeDtypeStruct((n_seq, CONV_PAD, CONV_DIM), F32)],
        scratch_shapes=[pltpu.VMEM((SSM_GROUPS, SSM_STATE, SSM_REP * SSM_HEAD_DIM), F32),
                        pltpu.VMEM((CONV_PAD + L, CONV_DIM), F32)],
        compiler_params=_params(2), name="ssd")(
            xbc, z, dt, conv_prev, h0, w_conv, b_conv.reshape(1, -1), pad_l(dt_bias), pad_l(a_log), dx,
            g_ssm.reshape(1, -1), e2, e1, tril)


def kernel(x_prompt, x_sample, cache_k, cache_v, state_ssm, state_conv, page_table, c_prompt, c_sample, w_ada, b_ada, g_norm1, w_in, q_norm_g, k_norm_g, w_conv, b_conv, dt_bias, a_log, d_skip, g_ssm, w_attn_o, w_ssm_o, w_out, g_norm2, w_up, w_down):
    n_batch, seq, d = x_prompt.shape
    n_dec, dec_seq, _ = x_sample.shape
    n_p, n_s = n_batch * seq, n_dec * dec_seq
    n_rows = n_p + n_s
    assert n_s == DEC_ROWS and d == D_MODEL and seq % KEY_GROUP == 0 and n_p % DEC_ROWS == 0
    n_pool = cache_k.shape[0]
    past_len = page_table.shape[1] * PAGE_SIZE
    n_blk = seq // MOBA_BLOCK
    sample_block = n_p // DEC_ROWS
    xp2, xs2 = x_prompt.reshape(n_p, D_MODEL), x_sample.reshape(n_s, D_MODEL)
    row1 = lambda v: v.reshape(1, -1)

    c_all = jnp.concatenate([c_prompt, c_sample], axis=0)
    c_pad = jnp.pad(c_all, ((0, -c_all.shape[0] % 16), (0, 0))).astype(BF16)
    mod = _mm(c_pad, w_ada, n_cols=6 * D_MODEL, n_i=1, tn=256, out_dtype=F32, epi=_epi_bias,
              extras=(row1(b_ada),), extra_specs=[pl.BlockSpec((1, 256), lambda i, j: (0, j))], name="ada_mod")
    mod = mod[:n_batch + n_dec].reshape(n_batch + n_dec, 6, D_MODEL)
    sh1, sc1, gt1, sh2, sc2, gt2 = (
        (mod[:n_batch, i].reshape(n_batch, 1, D_MODEL), jnp.repeat(mod[n_batch:, i], dec_seq, axis=0))
        for i in range(6))

    h = _norm_mod(xp2, xs2, 0, row1(g_norm1), sc1[0], sh1[0], sc1[1], sh1[1], n_p=n_p, rows_per_batch=seq,
                  name="norm_mod1")
    w_in_t = w_in.T
    proj = functools.partial(_mm, h, w_in_t, w_is_t=True, n_i=ROW_TILES, out_dtype=F32)
    q = proj(n_cols=Q_DIM, col_off=0, tn=256, epi=_epi_store, name="in_q")
    kv = proj(n_cols=2 * KV_DIM, col_off=Q_DIM, tn=256, epi=_epi_store, name="in_kv")
    z = proj(n_cols=SSM_INNER, col_off=Q_DIM + 2 * KV_DIM, tn=256, epi=_epi_store, name="in_z")
    xbc = proj(n_cols=CONV_DIM, col_off=Q_DIM + 2 * KV_DIM + SSM_INNER, tn=256, epi=_epi_store, name="in_xbc")
    dt_off = Q_DIM + 2 * KV_DIM + SSM_INNER + CONV_DIM
    dt = proj(n_cols=LANES, col_off=dt_off, tn=LANES, epi=_epi_store, name="in_dt")
    gates = proj(n_cols=2 * D_MODEL, col_off=dt_off, shift=SSM_HEADS, tn=256, epi=_epi_sigmoid, name="in_gates")

    cp, sp = _rope_tables(jnp.arange(seq, dtype=jnp.int32))
    k_p, k_aug, v_aug, kmean = _k_prep(kv, cp, sp, row1(k_norm_g), n_rows=n_p, n_blk=n_blk)
    q_aug = _q_prep(q, cp, sp, row1(q_norm_g), kmean.reshape(n_batch, n_blk, KV_DIM), n_rows=n_p, n_blk=n_blk)
    attn = _attn_prompt(q_aug, k_aug, v_aug, n_batch=n_batch, seq=seq, out_rows=n_rows)

    pos_s = past_len + (jnp.arange(n_s, dtype=jnp.int32) % dec_seq)
    cs, ss = _rope_tables(pos_s)
    q_s = _norm_rope_rows(q, sample_block, cs, ss, row1(q_norm_g), n_heads=N_HEADS, name="q_sample")
    k_s = _norm_rope_rows(kv, sample_block, cs, ss, row1(k_norm_g), n_heads=KV_HEADS, name="k_sample")
    v_s = kv[n_p:, KV_DIM:]
    q_t = q_s.reshape(n_dec, dec_seq, KV_HEADS, KV_REP, HEAD_DIM).transpose(0, 2, 4, 1, 3)
    q_t = q_t.reshape(n_dec, KV_HEADS, HEAD_DIM, GROUP_ROWS)
    eye = jnp.eye(KV_HEADS, dtype=F32)
    qbd_f32 = (q_t[:, :, :, None, :] * eye[None, :, None, :, None]).reshape(n_dec, KV_DIM, LANES)
    qbd = (qbd_f32 * ATTN_SCALE).astype(BF16)
    own_pad = lambda v: jnp.pad(v.reshape(n_dec, dec_seq, KV_DIM), ((0, 0), (0, 8 - dec_seq), (0, 0)))
    pool_view = lambda cache: cache.reshape(n_pool, PAGE_SIZE * KV_HEADS, HEAD_DIM)
    o_s = _attn_sample(page_table, pool_view(cache_k), pool_view(cache_v), qbd, qbd_f32, own_pad(k_s), own_pad(v_s),
                       dec_seq=dec_seq)
    attn_s = o_s.reshape(n_dec, KV_HEADS, dec_seq, KV_REP, HEAD_DIM).transpose(0, 2, 1, 3, 4)
    attn = _place_rows(attn, attn_s.reshape(n_s, Q_DIM).astype(BF16), sample_block, "attn_place")

    ssd = functools.partial(_ssd, w_conv=w_conv, b_conv=b_conv, dt_bias=dt_bias, a_log=a_log, d_skip=d_skip,
                            g_ssm=g_ssm)
    conv_pad = lambda v: jnp.pad(v, ((0, 0), (CONV_PAD - (CONV_W - 1), 0), (0, 0)))
    zeros_h = jnp.zeros((n_batch, SSM_HEADS, SSM_HEAD_DIM, SSM_STATE), F32)
    zeros_c = jnp.zeros((n_batch, CONV_W - 1, CONV_DIM), F32)
    y_ssm, ssm_p, cnew_p = ssd(xbc, z, dt, conv_pad(zeros_c), zeros_h, n_seq=n_batch, nc=seq // SSM_CHUNK,
                               rows_in=SSM_CHUNK, n_valid=SSM_CHUNK, out_rows=n_rows)
    pad8 = lambda v: jnp.pad(v[n_p:].reshape(n_dec, dec_seq, -1), ((0, 0), (0, 8 - dec_seq), (0, 0))).reshape(
        n_dec * 8, -1)
    y_s, ssm_s, cnew_s = ssd(pad8(xbc), pad8(z), pad8(dt), conv_pad(state_conv), state_ssm, n_seq=n_dec, nc=1,
                             rows_in=8, n_valid=dec_seq, out_rows=n_dec * 8)
    y_s = y_s.reshape(n_dec, 8, SSM_INNER)[:, :dec_seq].reshape(n_s, SSM_INNER)
    y_ssm = _place_rows(y_ssm, y_s, sample_block, "ssm_place")

    tm = n_rows // ROW_TILES
    tile = lambda off: pl.BlockSpec((tm, 256), lambda i, j: (i, j + off))
    mix_a = _mm(attn, w_attn_o, n_cols=D_MODEL, n_i=ROW_TILES, tn=256, out_dtype=F32, epi=_epi_gate,
                extras=(gates,), extra_specs=[tile(0)], name="attn_proj")
    mixed = _mm(y_ssm, w_ssm_o, n_cols=D_MODEL, n_i=ROW_TILES, tn=256, out_dtype=BF16, epi=_epi_gate_add,
                extras=(mix_a, gates), extra_specs=[tile(0), tile(D_MODEL // 256)], name="ssm_proj")
    tiles = _row_segments(tm, ROW_TILES, seq, n_batch, n_s)
    res_specs = [tile(0), pl.BlockSpec((DEC_ROWS, 256), lambda i, j: (0, j)),
                 pl.BlockSpec((n_batch, 1, 256), lambda i, j: (0, 0, j)),
                 pl.BlockSpec((DEC_ROWS, 256), lambda i, j: (0, j))]
    x1 = _mm(mixed, w_out, n_cols=D_MODEL, n_i=ROW_TILES, tn=256, out_dtype=F32, epi=_make_epi_residual(tiles),
             extras=(xp2, xs2, gt1[0], gt1[1]), extra_specs=res_specs, name="out_proj")

    h2 = _norm_mod(x1, x1, sample_block, row1(g_norm2), sc2[0], sh2[0], sc2[1], sh2[1], n_p=n_p,
                   rows_per_batch=seq, name="norm_mod2")
    u = _mm(h2, w_up, n_cols=D_FF, n_i=ROW_TILES, tn=256, out_dtype=BF16, epi=_epi_relu2, name="ffn_up")
    y_p, y_s2 = _ffn_down(u, w_down, x1, gt2[0], gt2[1], seq=seq, n_batch=n_batch, tn=512, tk=2048)

    kv4 = lambda v, b, t: v.reshape(b, t, KV_HEADS, HEAD_DIM)
    first = CONV_PAD - (CONV_W - 1)
    return (y_p.reshape(n_batch, seq, D_MODEL), y_s2.reshape(n_dec, dec_seq, D_MODEL),
            kv4(k_p, n_batch, seq), kv4(kv[:n_p, KV_DIM:], n_batch, seq),
            ssm_p, cnew_p[:, first:],
            kv4(k_s, n_dec, dec_seq), kv4(v_s, n_dec, dec_seq),
            ssm_s, cnew_s[:, first:])
```

```python
import functools

import jax
import jax.numpy as jnp
from jax import lax
from jax.experimental import pallas as pl
from jax.experimental.pallas import tpu as pltpu

F32, BF16 = jnp.float32, jnp.bfloat16

D_MODEL = 4096
HEAD_DIM = 128
N_HEADS = 32
KV_HEADS = 8
KV_REP = 4
MOBA_BLOCK = 256
MOBA_TOPK = 3
ROPE_THETA = 10000.0
PAGE_SIZE = 128
SSM_HEADS = 64
SSM_HEAD_DIM = 64
SSM_STATE = 128
SSM_GROUPS = 8
SSM_REP = 8
SSM_INNER = 4096
SSM_CHUNK = 128
CONV_W = 4
CONV_DIM = 6144
D_FF = 16384
EPS = 1e-6
Q_DIM = N_HEADS * HEAD_DIM
KV_DIM = KV_HEADS * HEAD_DIM
ATTN_SCALE = HEAD_DIM ** -0.5
MASK_SCORE = -1e30
VMEM_LIMIT_BYTES = 60 * 1024 * 1024
LANES = 128
DEC_ROWS = 128
ROW_TILES = 5
PROJ_TN = 512
GATED_TN = 256
DOWN_ROW_TILES = 4


def _params(n_axes):
    return pltpu.CompilerParams(dimension_semantics=("arbitrary",) * n_axes,
                                vmem_limit_bytes=VMEM_LIMIT_BYTES)


def _split3(x):
    hi = x.astype(BF16)
    r1 = x - hi.astype(F32)
    mid = r1.astype(BF16)
    lo = (r1 - mid.astype(F32)).astype(BF16)
    return hi, mid, lo


def _dot3_lhs_exact(w_exact, x):
    hi, mid, lo = _split3(x)
    f = functools.partial(jnp.dot, preferred_element_type=F32)
    return f(w_exact, lo) + f(w_exact, mid) + f(w_exact, hi)


def _dot3_rhs_exact(x, w_exact):
    hi, mid, lo = _split3(x)
    f = functools.partial(jnp.dot, preferred_element_type=F32)
    return f(lo, w_exact) + f(mid, w_exact) + f(hi, w_exact)


def _dot_hp(a, b, dims):
    a0, a1, a2 = _split3(a)
    b0, b1, b2 = _split3(b)
    f = functools.partial(lax.dot_general, dimension_numbers=dims, preferred_element_type=F32)
    return (f(a2, b0) + f(a1, b1) + f(a0, b2)) + (f(a1, b0) + f(a0, b1)) + f(a0, b0)


def _norm_rope(xh, g, c, s):
    xn = xh * lax.rsqrt(jnp.mean(xh * xh, axis=-1, keepdims=True) + EPS) * g
    return xn * c + pltpu.roll(xn, HEAD_DIM // 2, 1) * s


def _topk_select(gate, n_idx, n_blocks):
    cnt = jnp.zeros(gate.shape, F32)
    for m in range(n_blocks):
        rm = gate[m:m + 1, :]
        beats = jnp.where(rm > gate, 1.0, jnp.where((rm == gate) & (n_idx > m), 1.0, 0.0))
        cnt = cnt + beats
    return cnt < MOBA_TOPK


def _rope_tables(pos):
    half = HEAD_DIM // 2
    inv = ROPE_THETA ** (-jnp.arange(half, dtype=F32) / half)
    ang = pos.astype(F32)[:, None] * inv[None, :]
    c, s = jnp.cos(ang), jnp.sin(ang)
    return jnp.concatenate([c, c], axis=-1), jnp.concatenate([-s, s], axis=-1)


def _row_segments(tm, n_tiles, seq, n_batch, n_s):
    n_p = seq * n_batch
    tiles = []
    for t in range(n_tiles):
        lo, hi, cur = t * tm, (t + 1) * tm, []
        for b in range(n_batch):
            a, e = max(lo, b * seq), min(hi, (b + 1) * seq)
            if a < e:
                cur.append((a - lo, e - lo, "p", b))
        a, e = max(lo, n_p), min(hi, n_p + n_s)
        if a < e:
            assert (a, e) == (n_p, n_p + n_s) and (a - lo) % 8 == 0
            cur.append((a - lo, e - lo, "s", 0))
        tiles.append(cur)
    return tiles


def _per_tile(i, tiles, fn):
    for t, segs in enumerate(tiles):
        pl.when(i == t)(functools.partial(fn, segs))


def _epi_store(acc, i, ex, outs):
    outs[0][...] = acc.astype(outs[0].dtype)


def _epi_bias(acc, i, ex, outs):
    outs[0][...] = acc + ex[0][...]


def _epi_sigmoid(acc, i, ex, outs):
    outs[0][...] = jax.nn.sigmoid(acc)


def _epi_relu2(acc, i, ex, outs):
    r = jnp.maximum(acc, 0.0)
    outs[0][...] = (r * r).astype(outs[0].dtype)


def _epi_gate(acc, i, ex, outs):
    outs[0][...] = ex[0][...] * acc


def _epi_gate_add(acc, i, ex, outs):
    outs[0][...] = (ex[0][...] + ex[1][...] * acc).astype(outs[0].dtype)


def _make_epi_residual(tiles):
    def epi(acc, i, ex, outs):
        xp_ref, xs_ref, gp_ref, gs_ref = ex
        o_ref, = outs

        def write(segs):
            for r0, r1, kind, b in segs:
                if kind == "p":
                    o_ref[r0:r1, :] = xp_ref[r0:r1, :] + gp_ref[b] * acc[r0:r1]
                else:
                    o_ref[r0:r1, :] = xs_ref[...] + gs_ref[...] * acc[r0:r1]

        _per_tile(i, tiles, write)

    return epi


def _mm_kernel(a_ref, w_ref, *refs, n_ex, epi, shift, w_is_t):
    if shift:
        w = jnp.concatenate([w_ref[shift:, :], refs[0][...]], axis=0)
        refs = refs[1:]
    else:
        w = w_ref[...]
    dims = (((1,), (1,)), ((), ())) if w_is_t else (((1,), (0,)), ((), ()))
    acc = lax.dot_general(a_ref[...], w.astype(BF16), dims, preferred_element_type=F32)
    epi(acc, pl.program_id(0), refs[:n_ex], refs[n_ex:])


def _mm(a, w, *, n_cols, col_off=0, shift=0, w_is_t=False, n_i, tn, out_dtype, epi, extras=(), extra_specs=(),
        name):
    m, kdim = a.shape
    tm = m // n_i
    assert tm * n_i == m and col_off % tn == 0 and n_cols % tn == 0 and (not shift or w_is_t)
    off = col_off // tn
    if w_is_t:
        w_spec = pl.BlockSpec((tn, kdim), lambda i, j: (j + off, 0))
    else:
        w_spec = pl.BlockSpec((kdim, tn), lambda i, j: (0, j + off))
    in_specs = [pl.BlockSpec((tm, kdim), lambda i, j: (i, 0)), w_spec]
    args = [a, w]
    if shift:
        assert shift % 8 == 0 and tn % shift == 0
        per = tn // shift
        in_specs.append(pl.BlockSpec((shift, kdim), lambda i, j: ((j + off + 1) * per, 0)))
        args.append(w)
    kern = functools.partial(_mm_kernel, n_ex=len(extras), epi=epi, shift=shift, w_is_t=w_is_t)
    return pl.pallas_call(
        kern, grid=(n_i, n_cols // tn), in_specs=in_specs + list(extra_specs),
        out_specs=pl.BlockSpec((tm, tn), lambda i, j: (i, j)),
        out_shape=jax.ShapeDtypeStruct((m, n_cols), out_dtype),
        compiler_params=_params(2), name=name)(*args, *extras)


def _ffn_down_kernel(a_ref, w_ref, x_ref, gp_ref, gs_ref, yp_ref, ys_ref, *, nk, tiles):
    i, k = pl.program_id(0), pl.program_id(2)
    @pl.when(k == 0)
    def _():
        yp_ref[...] = jnp.zeros(yp_ref.shape, F32)

    yp_ref[...] += jnp.dot(a_ref[...], w_ref[...].astype(BF16), preferred_element_type=F32)

    @pl.when(k == nk - 1)
    def _():
        def write(segs):
            for r0, r1, kind, b in segs:
                if kind == "s":
                    ys_ref[...] = x_ref[r0:r1, :] + gs_ref[...] * yp_ref[r0:r1, :]
            for r0, r1, kind, b in segs:
                if kind == "p":
                    yp_ref[r0:r1, :] = x_ref[r0:r1, :] + gp_ref[b] * yp_ref[r0:r1, :]

        _per_tile(i, tiles, write)


def _ffn_down(u, w, x, gate_p, gate_s, *, seq, n_batch, tn, tk):
    m, kdim = u.shape
    n_i = DOWN_ROW_TILES
    tm, nk, n_j = m // n_i, kdim // tk, D_MODEL // tn
    n_p = seq * n_batch
    tiles = _row_segments(tm, n_i, seq, n_batch, m - n_p)
    kern = functools.partial(_ffn_down_kernel, nk=nk, tiles=tiles)
    return pl.pallas_call(
        kern, grid=(n_i, n_j, nk),
        in_specs=[pl.BlockSpec((tm, tk), lambda i, j, k: (i, k)),
                  pl.BlockSpec((tk, tn), lambda i, j, k: (k, j)),
                  pl.BlockSpec((tm, tn), lambda i, j, k: (i, j)),
                  pl.BlockSpec((n_batch, 1, tn), lambda i, j, k: (0, 0, j)),
                  pl.BlockSpec((DEC_ROWS, tn), lambda i, j, k: (0, j))],
        out_specs=[pl.BlockSpec((tm, tn), lambda i, j, k: (i, j)),
                   pl.BlockSpec((DEC_ROWS, tn), lambda i, j, k: (0, jnp.where(i == n_i - 1, j, 0)))],
        out_shape=[jax.ShapeDtypeStruct((n_p, D_MODEL), F32), jax.ShapeDtypeStruct((m - n_p, D_MODEL), F32)],
        compiler_params=_params(3), name="ffn_down")(u, w, x, gate_p, gate_s)


def _place_kernel(big_ref, small_ref, o_ref):
    o_ref[...] = small_ref[...]


def _place_rows(big, small, row_block, name):
    return pl.pallas_call(
        _place_kernel, grid=(1,),
        in_specs=[pl.BlockSpec(memory_space=pl.ANY), pl.BlockSpec(small.shape, lambda i: (0, 0))],
        out_specs=pl.BlockSpec(small.shape, lambda i: (row_block, 0)),
        out_shape=jax.ShapeDtypeStruct(big.shape, big.dtype),
        input_output_aliases={0: 0}, compiler_params=_params(1), name=name)(big, small)


NORM_ROWS = 512


def _norm_mod_kernel(xp_ref, xs_ref, g_ref, scp_ref, shp_ref, scs_ref, shs_ref, o_ref, *, n_i):
    i = pl.program_id(0)

    def apply(x, sc, sh):
        y = x * lax.rsqrt(jnp.mean(x * x, axis=-1, keepdims=True) + EPS) * g_ref[...]
        return (y * (1.0 + sc) + sh).astype(o_ref.dtype)

    @pl.when(i < n_i - 1)
    def _():
        o_ref[...] = apply(xp_ref[...], scp_ref[0], shp_ref[0])

    @pl.when(i == n_i - 1)
    def _():
        o_ref[:DEC_ROWS, :] = apply(xs_ref[...], scs_ref[...], shs_ref[...])


def _norm_mod(x_p, x_s, s_block, g, sc_p, sh_p, sc_s, sh_s, *, n_p, rows_per_batch, name):
    d = x_p.shape[1]
    tm = NORM_ROWS
    n_i = n_p // tm + 1
    tpb = rows_per_batch // tm
    nb = sc_p.shape[0]
    pspec = pl.BlockSpec((1, 1, d), lambda i: (jnp.minimum(i // tpb, nb - 1), 0, 0))
    sspec = pl.BlockSpec((DEC_ROWS, d), lambda i: (0, 0))
    return pl.pallas_call(
        functools.partial(_norm_mod_kernel, n_i=n_i), grid=(n_i,),
        in_specs=[pl.BlockSpec((tm, d), lambda i: (jnp.minimum(i, n_i - 2), 0)),
                  pl.BlockSpec((DEC_ROWS, d), lambda i: (s_block, 0)),
                  pl.BlockSpec((1, d), lambda i: (0, 0)), pspec, pspec, sspec, sspec],
        out_specs=pl.BlockSpec((tm, d), lambda i: (i, 0)),
        out_shape=jax.ShapeDtypeStruct((n_p + DEC_ROWS, d), BF16),
        compiler_params=_params(1), name=name)(x_p, x_s, g, sc_p, sh_p, sc_s, sh_s)


def _norm_rope_kernel(x_ref, c_ref, s_ref, g_ref, o_ref, *, n_heads):
    c, s, g = c_ref[...], s_ref[...], g_ref[...]
    for h in range(n_heads):
        sl = slice(h * HEAD_DIM, (h + 1) * HEAD_DIM)
        o_ref[:, sl] = _norm_rope(x_ref[:, sl], g, c, s)


def _norm_rope_rows(x, row_block, c, s, g, *, n_heads, name):
    w = n_heads * HEAD_DIM
    return pl.pallas_call(
        functools.partial(_norm_rope_kernel, n_heads=n_heads), grid=(1,),
        in_specs=[pl.BlockSpec((DEC_ROWS, w), lambda i: (row_block, 0)),
                  pl.BlockSpec((DEC_ROWS, HEAD_DIM), lambda i: (0, 0)),
                  pl.BlockSpec((DEC_ROWS, HEAD_DIM), lambda i: (0, 0)),
                  pl.BlockSpec((1, HEAD_DIM), lambda i: (0, 0))],
        out_specs=pl.BlockSpec((DEC_ROWS, w), lambda i: (0, 0)),
        out_shape=jax.ShapeDtypeStruct((DEC_ROWS, w), F32),
        compiler_params=_params(1), name=name)(x, c, s, g)


def _k_prep_kernel(k_ref, v_ref, c_ref, s_ref, g_ref, ko_ref, ka_ref, va_ref, km_ref, *, n_blk):
    blk = pl.program_id(0) % n_blk
    c, s, g = c_ref[...], s_ref[...], g_ref[...]
    lane = lax.broadcasted_iota(jnp.int32, (MOBA_BLOCK, HEAD_DIM), 1)
    block_tag = jnp.where(lane == blk, 1.0, 0.0).astype(BF16)
    ones_col = jnp.where(lane == 0, 1.0, 0.0).astype(BF16)
    for h in range(KV_HEADS):
        sl = slice(h * HEAD_DIM, (h + 1) * HEAD_DIM)
        lo, mid, hi = 2 * h * HEAD_DIM, (2 * h + 1) * HEAD_DIM, (2 * h + 2) * HEAD_DIM
        kr = _norm_rope(k_ref[:, sl], g, c, s)
        ko_ref[:, sl] = kr
        ka_ref[:, lo:mid] = kr.astype(BF16)
        ka_ref[:, mid:hi] = block_tag
        va_ref[:, lo:mid] = v_ref[:, sl].astype(BF16)
        va_ref[:, mid:hi] = ones_col
        km_ref[0, :, sl] = jnp.mean(kr, axis=0, keepdims=True)


def _k_prep(kv, c, s, g, *, n_rows, n_blk):
    nt = n_rows // MOBA_BLOCK
    return pl.pallas_call(
        functools.partial(_k_prep_kernel, n_blk=n_blk), grid=(nt,),
        in_specs=[pl.BlockSpec((MOBA_BLOCK, KV_DIM), lambda t: (t, 0)),
                  pl.BlockSpec((MOBA_BLOCK, KV_DIM), lambda t: (t, 1)),
                  pl.BlockSpec((MOBA_BLOCK, HEAD_DIM), lambda t: (t % n_blk, 0)),
                  pl.BlockSpec((MOBA_BLOCK, HEAD_DIM), lambda t: (t % n_blk, 0)),
                  pl.BlockSpec((1, HEAD_DIM), lambda t: (0, 0))],
        out_specs=[pl.BlockSpec((MOBA_BLOCK, KV_DIM), lambda t: (t, 0)),
                   pl.BlockSpec((MOBA_BLOCK, 2 * KV_DIM), lambda t: (t, 0)),
                   pl.BlockSpec((MOBA_BLOCK, 2 * KV_DIM), lambda t: (t, 0)),
                   pl.BlockSpec((1, 1, KV_DIM), lambda t: (t, 0, 0))],
        out_shape=[jax.ShapeDtypeStruct((n_rows, KV_DIM), F32),
                   jax.ShapeDtypeStruct((n_rows, 2 * KV_DIM), BF16),
                   jax.ShapeDtypeStruct((n_rows, 2 * KV_DIM), BF16),
                   jax.ShapeDtypeStruct((nt, 1, KV_DIM), F32)],
        compiler_params=_params(1), name="k_prep")(kv, kv, c, s, g)


def _q_prep_kernel(q_ref, c_ref, s_ref, g_ref, km_ref, o_ref, *, n_blk):
    own = pl.program_id(0) % n_blk
    c, s, g = c_ref[...], s_ref[...], g_ref[...]
    n_idx = lax.broadcasted_iota(jnp.int32, (n_blk, MOBA_BLOCK), 0)
    lane = lax.broadcasted_iota(jnp.int32, (MOBA_BLOCK, HEAD_DIM), 1)
    past = n_idx < own
    km = km_ref[0]
    pad = jnp.zeros((HEAD_DIM - n_blk, MOBA_BLOCK), F32)
    for h in range(N_HEADS):
        kv = h // KV_REP
        qh = _norm_rope(q_ref[:, h * HEAD_DIM:(h + 1) * HEAD_DIM], g, c, s)
        gate = _dot_hp(km[:, kv * HEAD_DIM:(kv + 1) * HEAD_DIM], qh, (((1,), (1,)), ((), ())))
        gate = jnp.where(past, gate, -jnp.inf)
        sel = (_topk_select(gate, n_idx, n_blk) & past) | (n_idx == own)
        sel_t = jnp.concatenate([jnp.where(sel, 1.0, 0.0), pad], axis=0).T
        bias = jnp.where(lane < n_blk, (sel_t - 1.0) * (-MASK_SCORE), 0.0)
        o_ref[:, 2 * h * HEAD_DIM:(2 * h + 1) * HEAD_DIM] = (qh * ATTN_SCALE).astype(BF16)
        o_ref[:, (2 * h + 1) * HEAD_DIM:(2 * h + 2) * HEAD_DIM] = bias.astype(BF16)


def _q_prep(q, c, s, g, kmean, *, n_rows, n_blk):
    nt = n_rows // MOBA_BLOCK
    return pl.pallas_call(
        functools.partial(_q_prep_kernel, n_blk=n_blk), grid=(nt,),
        in_specs=[pl.BlockSpec((MOBA_BLOCK, Q_DIM), lambda t: (t, 0)),
                  pl.BlockSpec((MOBA_BLOCK, HEAD_DIM), lambda t: (t % n_blk, 0)),
                  pl.BlockSpec((MOBA_BLOCK, HEAD_DIM), lambda t: (t % n_blk, 0)),
                  pl.BlockSpec((1, HEAD_DIM), lambda t: (0, 0)),
                  pl.BlockSpec((1, n_blk, KV_DIM), lambda t: (t // n_blk, 0, 0))],
        out_specs=pl.BlockSpec((MOBA_BLOCK, 2 * Q_DIM), lambda t: (t, 0)),
        out_shape=jax.ShapeDtypeStruct((n_rows, 2 * Q_DIM), BF16),
        compiler_params=_params(1), name="q_prep")(q, c, s, g, kmean)


KEY_GROUP = 4 * MOBA_BLOCK


def _attn_p_kernel(q_ref, k_ref, v_ref, o_ref):
    i = pl.program_id(2)
    rows = KV_REP * MOBA_BLOCK
    aug = 2 * HEAD_DIM
    qs = jnp.concatenate([q_ref[:, r * aug:(r + 1) * aug] for r in range(KV_REP)], axis=0)
    dims = (((1,), (1,)), ((), ()))
    n_full = (i * MOBA_BLOCK) // KEY_GROUP

    def scores(gi):
        st = pl.multiple_of(gi * KEY_GROUP, KEY_GROUP)
        s = lax.dot_general(qs, k_ref[pl.ds(st, KEY_GROUP), :], dims, preferred_element_type=F32)
        return s, v_ref[pl.ds(st, KEY_GROUP), :]

    s, vg = scores(n_full)
    qpos = i * MOBA_BLOCK + (lax.broadcasted_iota(jnp.int32, (rows, KEY_GROUP), 0) & (MOBA_BLOCK - 1))
    kpos = n_full * KEY_GROUP + lax.broadcasted_iota(jnp.int32, (rows, KEY_GROUP), 1)
    s = jnp.where(kpos <= qpos, s, -jnp.inf)
    m = jnp.max(s, axis=1, keepdims=True)
    acc = jnp.dot(jnp.exp(s - m).astype(BF16), vg, preferred_element_type=F32)

    def body(gi, carry):
        m, acc = carry
        s, vg = scores(gi)
        m_new = jnp.maximum(m, jnp.max(s, axis=1, keepdims=True))
        acc = acc * jnp.exp(m - m_new) + jnp.dot(jnp.exp(s - m_new).astype(BF16), vg, preferred_element_type=F32)
        return m_new, acc

    m, acc = lax.fori_loop(0, n_full, body, (m, acc))
    o = acc[:, :HEAD_DIM] / acc[:, HEAD_DIM:HEAD_DIM + 1]
    for r in range(KV_REP):
        o_ref[:, r * HEAD_DIM:(r + 1) * HEAD_DIM] = o[r * MOBA_BLOCK:(r + 1) * MOBA_BLOCK].astype(o_ref.dtype)


def _attn_prompt(q_aug, k_aug, v_aug, *, n_batch, seq, out_rows):
    n_blk = seq // MOBA_BLOCK
    aug = 2 * HEAD_DIM
    return pl.pallas_call(
        _attn_p_kernel, grid=(n_batch, KV_HEADS, n_blk),
        in_specs=[pl.BlockSpec((MOBA_BLOCK, KV_REP * aug), lambda b, g, i: (b * n_blk + i, g)),
                  pl.BlockSpec((seq, aug), lambda b, g, i: (b, g)),
                  pl.BlockSpec((seq, aug), lambda b, g, i: (b, g))],
        out_specs=pl.BlockSpec((MOBA_BLOCK, KV_REP * HEAD_DIM), lambda b, g, i: (b * n_blk + i, g)),
        out_shape=jax.ShapeDtypeStruct((out_rows, Q_DIM), BF16),
        compiler_params=_params(3), name="attn_prompt")(q_aug, k_aug, v_aug)


GROUP_ROWS = 16
PAGES_PER_STEP = 8


def _diag_blocks(full):
    return jnp.concatenate([full[g * GROUP_ROWS:(g + 1) * GROUP_ROWS, g * HEAD_DIM:(g + 1) * HEAD_DIM]
                            for g in range(KV_HEADS)], axis=0)


def _page_tile(ref):
    return jnp.concatenate([ref[0, pl.ds(g, PAGE_SIZE, stride=KV_HEADS), :] for g in range(KV_HEADS)], axis=1)


def _attn_s_kernel(pt_ref, *refs, n_blk, dec_seq):
    n_pg = PAGES_PER_STEP
    k_refs, v_refs = refs[:n_pg], refs[n_pg:2 * n_pg]
    qb_ref, qf_ref, ko_ref, vo_ref, o_ref, ob_ref, m_ref, l_ref, ks_ref = refs[2 * n_pg:]
    step = pl.program_id(1)
    pages_per_block = MOBA_BLOCK // PAGE_SIZE
    blocks_per_step = n_pg // pages_per_block
    qb = qb_ref[0]
    for u in range(blocks_per_step):
        jb = step * blocks_per_step + u
        pages = range(u * pages_per_block, (u + 1) * pages_per_block)
        kblk = jnp.concatenate([_page_tile(k_refs[p]) for p in pages], axis=0)
        vblk = jnp.concatenate([_page_tile(v_refs[p]) for p in pages], axis=0).astype(BF16)
        ks_ref[pl.ds(jb, 1), :] = jnp.sum(kblk, axis=0, keepdims=True)
        st = jnp.dot(kblk.astype(BF16), qb, preferred_element_type=F32)
        mb = jnp.max(st, axis=0, keepdims=True)
        p = jnp.exp(st - mb)
        m_ref[pl.ds(jb, 1), :] = mb
        l_ref[pl.ds(jb, 1), :] = jnp.sum(p, axis=0, keepdims=True)
        ob_ref[jb] = _diag_blocks(jnp.dot(p.T.astype(BF16), vblk, preferred_element_type=F32))

    @pl.when(step == n_blk // blocks_per_step - 1)
    def _():
        gate = _dot_hp(ks_ref[...] * (1.0 / MOBA_BLOCK), qf_ref[0], (((1,), (0,)), ((), ())))
        n_idx = lax.broadcasted_iota(jnp.int32, (n_blk, LANES), 0)
        sel = _topk_select(gate, n_idx, n_blk)
        m_all = m_ref[...]
        so = jnp.dot(ko_ref[0].astype(BF16), qb, preferred_element_type=F32)
        krow = lax.broadcasted_iota(jnp.int32, so.shape, 0)
        tq = (lax.broadcasted_iota(jnp.int32, so.shape, 1) >> 2) & (dec_seq - 1)
        so = jnp.where((krow <= tq) & (krow < dec_seq), so, -jnp.inf)
        m_tot = jnp.maximum(jnp.max(jnp.where(sel, m_all, -jnp.inf), axis=0, keepdims=True),
                            jnp.max(so, axis=0, keepdims=True))
        w = jnp.where(sel, jnp.exp(m_all - m_tot), 0.0)
        po = jnp.exp(so - m_tot)
        inv = 1.0 / (jnp.sum(w * l_ref[...], axis=0, keepdims=True) + jnp.sum(po, axis=0, keepdims=True))
        pad = jnp.zeros((LANES - n_blk - so.shape[0], LANES), F32)
        wt = jnp.concatenate([w * inv, po * inv, pad], axis=0).T
        acc = jnp.zeros((LANES, HEAD_DIM), F32)
        for b in range(n_blk):
            acc = acc + wt[:, b:b + 1] * ob_ref[b]
        vo = vo_ref[0]
        for t in range(dec_seq):
            vrow = jnp.concatenate(
                [jnp.broadcast_to(vo[t:t + 1, g * HEAD_DIM:(g + 1) * HEAD_DIM], (GROUP_ROWS, HEAD_DIM))
                 for g in range(KV_HEADS)], axis=0)
            acc = acc + wt[:, n_blk + t:n_blk + t + 1] * vrow
        o_ref[0] = acc


def _attn_sample(page_table, cache_k, cache_v, qbd, qbd_f32, k_own, v_own, *, dec_seq):
    n_seq, n_pages = page_table.shape
    n_blk = n_pages * PAGE_SIZE // MOBA_BLOCK
    page_rows = PAGE_SIZE * KV_HEADS
    n_pg = PAGES_PER_STEP
    assert n_pages % n_pg == 0
    pages = [pl.BlockSpec((1, page_rows, HEAD_DIM), lambda s, t, pt, p=p: (pt[s, t * n_pg + p], 0, 0))
             for p in range(n_pg)]
    per_seq = lambda r, c: pl.BlockSpec((1, r, c), lambda s, t, pt: (s, 0, 0))
    grid_spec = pltpu.PrefetchScalarGridSpec(
        num_scalar_prefetch=1, grid=(n_seq, n_pages // n_pg),
        in_specs=pages + pages + [per_seq(KV_DIM, LANES), per_seq(KV_DIM, LANES),
                                  per_seq(8, KV_DIM), per_seq(8, KV_DIM)],
        out_specs=per_seq(LANES, HEAD_DIM),
        scratch_shapes=[pltpu.VMEM((n_blk, LANES, HEAD_DIM), F32), pltpu.VMEM((n_blk, LANES), F32),
                        pltpu.VMEM((n_blk, LANES), F32), pltpu.VMEM((n_blk, KV_DIM), F32)])
    return pl.pallas_call(
        functools.partial(_attn_s_kernel, n_blk=n_blk, dec_seq=dec_seq), grid_spec=grid_spec,
        out_shape=jax.ShapeDtypeStruct((n_seq, LANES, HEAD_DIM), F32),
        compiler_params=_params(2), name="attn_sample")(
            page_table, *([cache_k] * n_pg), *([cache_v] * n_pg), qbd, qbd_f32, k_own, v_own)


CONV_PAD = 8


def _ssd_kernel(xbc_ref, z_ref, dt_ref, cprev_ref, h0_ref, wc_ref, bc_ref, dtb_ref, alog_ref, dx_ref, gs_ref,
                e2_ref, e1_ref, tril_ref, y_ref, hf_ref, cnew_ref, ht_ref, cbuf_ref, *, L, nc, n_valid, rows_in):
    c = pl.program_id(1)
    gw = SSM_REP * SSM_HEAD_DIM

    @pl.when(c == 0)
    def _():
        cbuf_ref[0:CONV_PAD, :] = cprev_ref[0]
        if rows_in < L:
            cbuf_ref[CONV_PAD + rows_in:, :] = jnp.zeros((L - rows_in, CONV_DIM), F32)
        for g in range(SSM_GROUPS):
            ht_ref[g] = h0_ref[0, g * SSM_REP:(g + 1) * SSM_REP].reshape(gw, SSM_STATE).T

    cbuf_ref[CONV_PAD:CONV_PAD + rows_in, :] = xbc_ref[...]
    first = CONV_PAD - (CONV_W - 1)
    conv = bc_ref[...]
    for j in range(CONV_W):
        conv = conv + cbuf_ref[first + j:first + j + L, :] * wc_ref[j:j + 1, :]
    cnew_ref[0] = cbuf_ref[n_valid:CONV_PAD + n_valid, :]
    cbuf_ref[first:CONV_PAD, :] = cbuf_ref[first + L:CONV_PAD + L, :]

    act = conv * jax.nn.sigmoid(conv)
    xs = act[:, :SSM_INNER]
    if rows_in < L:
        zpad = jnp.zeros((L - rows_in, SSM_INNER), F32)
        z = jnp.concatenate([z_ref[...], zpad], axis=0)
        dt_raw = jnp.concatenate([dt_ref[...], jnp.zeros((L - rows_in, LANES), F32)], axis=0)
    else:
        z, dt_raw = z_ref[...], dt_ref[...]

    row = lax.broadcasted_iota(jnp.int32, (L, LANES), 0)
    lane = lax.broadcasted_iota(jnp.int32, (L, LANES), 1)
    dt = jax.nn.softplus(dt_raw + dtb_ref[...])
    dt = jnp.where((row < n_valid) & (lane < SSM_HEADS), dt, 0.0)
    cum = _dot3_lhs_exact(tril_ref[...], dt * (-jnp.exp(alog_ref[...])))
    cum_t, dt_t = cum.T, dt.T
    causal = lax.broadcasted_iota(jnp.int32, (L, L), 1) <= lax.broadcasted_iota(jnp.int32, (L, L), 0)

    outs = []
    for g in range(SSM_GROUPS):
        sl = slice(g * gw, (g + 1) * gw)
        bg = act[:, SSM_INNER + g * SSM_STATE:SSM_INNER + (g + 1) * SSM_STATE]
        cg = act[:, SSM_INNER + (SSM_GROUPS + g) * SSM_STATE:SSM_INNER + (SSM_GROUPS + g + 1) * SSM_STATE]
        bg_b, cg_b = bg.astype(BF16), cg.astype(BF16)
        cb = lax.dot_general(cg_b, bg_b, (((1,), (1,)), ((), ())), preferred_element_type=F32)
        cum_x2 = _dot3_rhs_exact(cum, e2_ref[:, g * SSM_REP * LANES:(g + 1) * SSM_REP * LANES])
        cum_x = _dot3_rhs_exact(cum, e1_ref[:, sl])
        dt_x = _dot3_rhs_exact(dt, e1_ref[:, sl])
        ht = ht_ref[g]
        y_inter = jnp.dot(cg_b, ht.astype(BF16), preferred_element_type=F32)
        parts = []
        for r in range(SSM_REP):
            h = g * SSM_REP + r
            seg = cum_x2[:, r * LANES:(r + 1) * LANES] - cum_t[h:h + 1, :]
            w = cb * jnp.exp(jnp.where(causal, seg, -jnp.inf)) * dt_t[h:h + 1, :]
            xh = xs[:, h * SSM_HEAD_DIM:(h + 1) * SSM_HEAD_DIM].astype(BF16)
            parts.append(jnp.dot(w.astype(BF16), xh, preferred_element_type=F32))
        xg = xs[:, sl]
        yg = jnp.concatenate(parts, axis=1) + y_inter * jnp.exp(cum_x) + dx_ref[:, sl] * xg
        last = cum_x[L - 1:L, :]
        xw = (xg * (jnp.exp(last - cum_x) * dt_x)).astype(BF16)
        ht_ref[g] = ht * jnp.exp(last) + jnp.dot(bg.T.astype(BF16), xw, preferred_element_type=F32)
        zg = z[:, sl]
        yg = yg * (zg * jax.nn.sigmoid(zg))
        yg = yg * lax.rsqrt(jnp.mean(yg * yg, axis=-1, keepdims=True) + EPS) * gs_ref[:, sl]
        outs.append(yg[:rows_in].astype(y_ref.dtype))
    y_ref[...] = jnp.concatenate(outs, axis=1)

    @pl.when(c == nc - 1)
    def _():
        for g in range(SSM_GROUPS):
            hf_ref[0, g * SSM_REP:(g + 1) * SSM_REP] = ht_ref[g].T.reshape(SSM_REP, SSM_HEAD_DIM, SSM_STATE)


def _ssd(xbc, z, dt, conv_prev, h0, w_conv, b_conv, dt_bias, a_log, d_skip, g_ssm, *, n_seq, nc, rows_in, n_valid,
         out_rows):
    L = SSM_CHUNK
    head_of = jnp.arange(SSM_HEADS * LANES, dtype=jnp.int32) // LANES
    e2 = (jnp.arange(LANES, dtype=jnp.int32)[:, None] == head_of[None, :]).astype(BF16)
    chan_head = jnp.arange(SSM_INNER, dtype=jnp.int32) // SSM_HEAD_DIM
    e1 = (jnp.arange(LANES, dtype=jnp.int32)[:, None] == chan_head[None, :]).astype(BF16)
    tril = (jnp.arange(L)[:, None] >= jnp.arange(L)[None, :]).astype(BF16)
    pad_l = lambda v: jnp.pad(v.reshape(1, -1), ((0, 0), (0, LANES - v.shape[0])))
    dx = jnp.repeat(d_skip, SSM_HEAD_DIM).reshape(1, SSM_INNER)
    const = lambda shape: pl.BlockSpec(shape, lambda b, c: (0,) * len(shape))
    kern = functools.partial(_ssd_kernel, L=L, nc=nc, n_valid=n_valid, rows_in=rows_in)
    return pl.pallas_call(
        kern, grid=(n_seq, nc),
        in_specs=[pl.BlockSpec((rows_in, CONV_DIM), lambda b, c: (b * nc + c, 0)),
                  pl.BlockSpec((rows_in, SSM_INNER), lambda b, c: (b * nc + c, 0)),
                  pl.BlockSpec((rows_in, LANES), lambda b, c: (b * nc + c, 0)),
                  pl.BlockSpec((1, CONV_PAD, CONV_DIM), lambda b, c: (b, 0, 0)),
                  pl.BlockSpec((1, SSM_HEADS, SSM_HEAD_DIM, SSM_STATE), lambda b, c: (b, 0, 0, 0)),
                  const((CONV_W, CONV_DIM)), const((1, CONV_DIM)), const((1, LANES)), const((1, LANES)),
                  const((1, SSM_INNER)), const((1, SSM_INNER)),
                  const((LANES, SSM_HEADS * LANES)), const((LANES, SSM_INNER)), const((L, L))],
        out_specs=[pl.BlockSpec((rows_in, SSM_INNER), lambda b, c: (b * nc + c, 0)),
                   pl.BlockSpec((1, SSM_HEADS, SSM_HEAD_DIM, SSM_STATE), lambda b, c: (b, 0, 0, 0)),
                   pl.BlockSpec((1, CONV_PAD, CONV_DIM), lambda b, c: (b, 0, 0))],
        out_shape=[jax.ShapeDtypeStruct((out_rows, SSM_INNER), BF16),
                   jax.ShapeDtypeStruct((n_seq, SSM_HEADS, SSM_HEAD_DIM, SSM_STATE), F32),
                   jax.ShapeDtypeStruct((n_seq, CONV_PAD, CONV_DIM), F32)],
        scratch_shapes=[pltpu.VMEM((SSM_GROUPS, SSM_STATE, SSM_REP * SSM_HEAD_DIM), F32),
                        pltpu.VMEM((CONV_PAD + L, CONV_DIM), F32)],
        compiler_params=_params(2), name="ssd")(
            xbc, z, dt, conv_prev, h0, w_conv, b_conv.reshape(1, -1), pad_l(dt_bias), pad_l(a_log), dx,
            g_ssm.reshape(1, -1), e2, e1, tril)


def kernel(x_prompt, x_sample, cache_k, cache_v, state_ssm, state_conv, page_table, c_prompt, c_sample, w_ada, b_ada, g_norm1, w_in, q_norm_g, k_norm_g, w_conv, b_conv, dt_bias, a_log, d_skip, g_ssm, w_attn_o, w_ssm_o, w_out, g_norm2, w_up, w_down):
    n_batch, seq, d = x_prompt.shape
    n_dec, dec_seq, _ = x_sample.shape
    n_p, n_s = n_batch * seq, n_dec * dec_seq
    n_rows = n_p + n_s
    assert n_s == DEC_ROWS and d == D_MODEL and seq % KEY_GROUP == 0 and n_p % DEC_ROWS == 0
    n_pool = cache_k.shape[0]
    past_len = page_table.shape[1] * PAGE_SIZE
    n_blk = seq // MOBA_BLOCK
    sample_block = n_p // DEC_ROWS
    xp2, xs2 = x_prompt.reshape(n_p, D_MODEL), x_sample.reshape(n_s, D_MODEL)
    row1 = lambda v: v.reshape(1, -1)

    c_all = jnp.concatenate([c_prompt, c_sample], axis=0)
    c_pad = jnp.pad(c_all, ((0, -c_all.shape[0] % 16), (0, 0))).astype(BF16)
    tn = PROJ_TN
    mod = _mm(c_pad, w_ada, n_cols=6 * D_MODEL, n_i=1, tn=tn, out_dtype=F32, epi=_epi_bias,
              extras=(row1(b_ada),), extra_specs=[pl.BlockSpec((1, tn), lambda i, j: (0, j))], name="ada_mod")
    mod = mod[:n_batch + n_dec].reshape(n_batch + n_dec, 6, D_MODEL)
    sh1, sc1, gt1, sh2, sc2, gt2 = (
        (mod[:n_batch, i].reshape(n_batch, 1, D_MODEL), jnp.repeat(mod[n_batch:, i], dec_seq, axis=0))
        for i in range(6))

    h = _norm_mod(xp2, xs2, 0, row1(g_norm1), sc1[0], sh1[0], sc1[1], sh1[1], n_p=n_p, rows_per_batch=seq,
                  name="norm_mod1")
    w_in_t = w_in.T
    proj = functools.partial(_mm, h, w_in_t, w_is_t=True, n_i=ROW_TILES, out_dtype=F32)
    q = proj(n_cols=Q_DIM, col_off=0, tn=tn, epi=_epi_store, name="in_q")
    kv = proj(n_cols=2 * KV_DIM, col_off=Q_DIM, tn=tn, epi=_epi_store, name="in_kv")
    z = proj(n_cols=SSM_INNER, col_off=Q_DIM + 2 * KV_DIM, tn=tn, epi=_epi_store, name="in_z")
    xbc = proj(n_cols=CONV_DIM, col_off=Q_DIM + 2 * KV_DIM + SSM_INNER, tn=tn, epi=_epi_store, name="in_xbc")
    dt_off = Q_DIM + 2 * KV_DIM + SSM_INNER + CONV_DIM
    dt = proj(n_cols=LANES, col_off=dt_off, tn=LANES, epi=_epi_store, name="in_dt")
    gates = proj(n_cols=2 * D_MODEL, col_off=dt_off, shift=SSM_HEADS, tn=tn, epi=_epi_sigmoid, name="in_gates")

    cp, sp = _rope_tables(jnp.arange(seq, dtype=jnp.int32))
    k_p, k_aug, v_aug, kmean = _k_prep(kv, cp, sp, row1(k_norm_g), n_rows=n_p, n_blk=n_blk)
    q_aug = _q_prep(q, cp, sp, row1(q_norm_g), kmean.reshape(n_batch, n_blk, KV_DIM), n_rows=n_p, n_blk=n_blk)
    attn = _attn_prompt(q_aug, k_aug, v_aug, n_batch=n_batch, seq=seq, out_rows=n_rows)

    pos_s = past_len + (jnp.arange(n_s, dtype=jnp.int32) % dec_seq)
    cs, ss = _rope_tables(pos_s)
    q_s = _norm_rope_rows(q, sample_block, cs, ss, row1(q_norm_g), n_heads=N_HEADS, name="q_sample")
    k_s = _norm_rope_rows(kv, sample_block, cs, ss, row1(k_norm_g), n_heads=KV_HEADS, name="k_sample")
    v_s = kv[n_p:, KV_DIM:]
    q_t = q_s.reshape(n_dec, dec_seq, KV_HEADS, KV_REP, HEAD_DIM).transpose(0, 2, 4, 1, 3)
    q_t = q_t.reshape(n_dec, KV_HEADS, HEAD_DIM, GROUP_ROWS)
    eye = jnp.eye(KV_HEADS, dtype=F32)
    qbd_f32 = (q_t[:, :, :, None, :] * eye[None, :, None, :, None]).reshape(n_dec, KV_DIM, LANES)
    qbd = (qbd_f32 * ATTN_SCALE).astype(BF16)
    own_pad = lambda v: jnp.pad(v.reshape(n_dec, dec_seq, KV_DIM), ((0, 0), (0, 8 - dec_seq), (0, 0)))
    pool_view = lambda cache: cache.reshape(n_pool, PAGE_SIZE * KV_HEADS, HEAD_DIM)
    o_s = _attn_sample(page_table, pool_view(cache_k), pool_view(cache_v), qbd, qbd_f32, own_pad(k_s), own_pad(v_s),
                       dec_seq=dec_seq)
    attn_s = o_s.reshape(n_dec, KV_HEADS, dec_seq, KV_REP, HEAD_DIM).transpose(0, 2, 1, 3, 4)
    attn = _place_rows(attn, attn_s.reshape(n_s, Q_DIM).astype(BF16), sample_block, "attn_place")

    ssd = functools.partial(_ssd, w_conv=w_conv, b_conv=b_conv, dt_bias=dt_bias, a_log=a_log, d_skip=d_skip,
                            g_ssm=g_ssm)
    conv_pad = lambda v: jnp.pad(v, ((0, 0), (CONV_PAD - (CONV_W - 1), 0), (0, 0)))
    zeros_h = jnp.zeros((n_batch, SSM_HEADS, SSM_HEAD_DIM, SSM_STATE), F32)
    zeros_c = jnp.zeros((n_batch, CONV_W - 1, CONV_DIM), F32)
    y_ssm, ssm_p, cnew_p = ssd(xbc, z, dt, conv_pad(zeros_c), zeros_h, n_seq=n_batch, nc=seq // SSM_CHUNK,
                               rows_in=SSM_CHUNK, n_valid=SSM_CHUNK, out_rows=n_rows)
    pad8 = lambda v: jnp.pad(v[n_p:].reshape(n_dec, dec_seq, -1), ((0, 0), (0, 8 - dec_seq), (0, 0))).reshape(
        n_dec * 8, -1)
    y_s, ssm_s, cnew_s = ssd(pad8(xbc), pad8(z), pad8(dt), conv_pad(state_conv), state_ssm, n_seq=n_dec, nc=1,
                             rows_in=8, n_valid=dec_seq, out_rows=n_dec * 8)
    y_s = y_s.reshape(n_dec, 8, SSM_INNER)[:, :dec_seq].reshape(n_s, SSM_INNER)
    y_ssm = _place_rows(y_ssm, y_s, sample_block, "ssm_place")

    tm = n_rows // ROW_TILES
    tn = GATED_TN
    tile = lambda off: pl.BlockSpec((tm, tn), lambda i, j: (i, j + off))
    mix_a = _mm(attn, w_attn_o, n_cols=D_MODEL, n_i=ROW_TILES, tn=tn, out_dtype=F32, epi=_epi_gate,
                extras=(gates,), extra_specs=[tile(0)], name="attn_proj")
    mixed = _mm(y_ssm, w_ssm_o, n_cols=D_MODEL, n_i=ROW_TILES, tn=tn, out_dtype=BF16, epi=_epi_gate_add,
                extras=(mix_a, gates), extra_specs=[tile(0), tile(D_MODEL // tn)], name="ssm_proj")
    tiles = _row_segments(tm, ROW_TILES, seq, n_batch, n_s)
    res_specs = [tile(0), pl.BlockSpec((DEC_ROWS, tn), lambda i, j: (0, j)),
                 pl.BlockSpec((n_batch, 1, tn), lambda i, j: (0, 0, j)),
                 pl.BlockSpec((DEC_ROWS, tn), lambda i, j: (0, j))]
    x1 = _mm(mixed, w_out, n_cols=D_MODEL, n_i=ROW_TILES, tn=tn, out_dtype=F32, epi=_make_epi_residual(tiles),
             extras=(xp2, xs2, gt1[0], gt1[1]), extra_specs=res_specs, name="out_proj")

    h2 = _norm_mod(x1, x1, sample_block, row1(g_norm2), sc2[0], sh2[0], sc2[1], sh2[1], n_p=n_p,
                   rows_per_batch=seq, name="norm_mod2")
    u = _mm(h2, w_up, n_cols=D_FF, n_i=ROW_TILES, tn=PROJ_TN, out_dtype=BF16, epi=_epi_relu2, name="ffn_up")
    y_p, y_s2 = _ffn_down(u, w_down, x1, gt2[0], gt2[1], seq=seq, n_batch=n_batch, tn=512, tk=2048)

    kv4 = lambda v, b, t: v.reshape(b, t, KV_HEADS, HEAD_DIM)
    first = CONV_PAD - (CONV_W - 1)
    return (y_p.reshape(n_batch, seq, D_MODEL), y_s2.reshape(n_dec, dec_seq, D_MODEL),
            kv4(k_p, n_batch, seq), kv4(kv[:n_p, KV_DIM:], n_batch, seq),
            ssm_p, cnew_p[:, first:],
            kv4(k_s, n_dec, dec_seq), kv4(v_s, n_dec, dec_seq),
            ssm_s, cnew_s[:, first:])
```

```python
import functools

import jax
import jax.numpy as jnp
from jax import lax
from jax.experimental import pallas as pl
from jax.experimental.pallas import tpu as pltpu

F32, BF16 = jnp.float32, jnp.bfloat16

D_MODEL = 4096
HEAD_DIM = 128
N_HEADS = 32
KV_HEADS = 8
KV_REP = 4
MOBA_BLOCK = 256
MOBA_TOPK = 3
ROPE_THETA = 10000.0
PAGE_SIZE = 128
SSM_HEADS = 64
SSM_HEAD_DIM = 64
SSM_STATE = 128
SSM_GROUPS = 8
SSM_REP = 8
SSM_INNER = 4096
SSM_CHUNK = 128
CONV_W = 4
CONV_DIM = 6144
D_FF = 16384
EPS = 1e-6
Q_DIM = N_HEADS * HEAD_DIM
KV_DIM = KV_HEADS * HEAD_DIM
ATTN_SCALE = HEAD_DIM ** -0.5
MASK_SCORE = -1e30
VMEM_LIMIT_BYTES = 60 * 1024 * 1024
LANES = 128
DEC_ROWS = 128
ROW_TILES = 5
PROJ_TN = 512
GATED_ROW_TILES = 8
DOWN_ROW_TILES = 4


def _params(n_axes):
    return pltpu.CompilerParams(dimension_semantics=("arbitrary",) * n_axes,
                                vmem_limit_bytes=VMEM_LIMIT_BYTES)


def _split3(x):
    hi = x.astype(BF16)
    r1 = x - hi.astype(F32)
    mid = r1.astype(BF16)
    lo = (r1 - mid.astype(F32)).astype(BF16)
    return hi, mid, lo


def _dot3_lhs_exact(w_exact, x):
    hi, mid, lo = _split3(x)
    f = functools.partial(jnp.dot, preferred_element_type=F32)
    return f(w_exact, lo) + f(w_exact, mid) + f(w_exact, hi)


def _dot3_rhs_exact(x, w_exact):
    hi, mid, lo = _split3(x)
    f = functools.partial(jnp.dot, preferred_element_type=F32)
    return f(lo, w_exact) + f(mid, w_exact) + f(hi, w_exact)


def _dot_hp(a, b, dims):
    a0, a1, a2 = _split3(a)
    b0, b1, b2 = _split3(b)
    f = functools.partial(lax.dot_general, dimension_numbers=dims, preferred_element_type=F32)
    return (f(a2, b0) + f(a1, b1) + f(a0, b2)) + (f(a1, b0) + f(a0, b1)) + f(a0, b0)


def _norm_rope(xh, g, c, s):
    xn = xh * lax.rsqrt(jnp.mean(xh * xh, axis=-1, keepdims=True) + EPS) * g
    return xn * c + pltpu.roll(xn, HEAD_DIM // 2, 1) * s


def _topk_select(gate, n_idx, n_blocks):
    cnt = jnp.zeros(gate.shape, F32)
    for m in range(n_blocks):
        rm = gate[m:m + 1, :]
        beats = jnp.where(rm > gate, 1.0, jnp.where((rm == gate) & (n_idx > m), 1.0, 0.0))
        cnt = cnt + beats
    return cnt < MOBA_TOPK


def _rope_tables(pos):
    half = HEAD_DIM // 2
    inv = ROPE_THETA ** (-jnp.arange(half, dtype=F32) / half)
    ang = pos.astype(F32)[:, None] * inv[None, :]
    c, s = jnp.cos(ang), jnp.sin(ang)
    return jnp.concatenate([c, c], axis=-1), jnp.concatenate([-s, s], axis=-1)


def _row_segments(tm, n_tiles, seq, n_batch, n_s):
    n_p = seq * n_batch
    tiles = []
    for t in range(n_tiles):
        lo, hi, cur = t * tm, (t + 1) * tm, []
        for b in range(n_batch):
            a, e = max(lo, b * seq), min(hi, (b + 1) * seq)
            if a < e:
                cur.append((a - lo, e - lo, "p", b))
        a, e = max(lo, n_p), min(hi, n_p + n_s)
        if a < e:
            assert (a, e) == (n_p, n_p + n_s) and (a - lo) % 8 == 0
            cur.append((a - lo, e - lo, "s", 0))
        tiles.append(cur)
    return tiles


def _per_tile(i, tiles, fn):
    for t, segs in enumerate(tiles):
        pl.when(i == t)(functools.partial(fn, segs))


def _epi_store(acc, i, ex, outs):
    outs[0][...] = acc.astype(outs[0].dtype)


def _epi_bias(acc, i, ex, outs):
    outs[0][...] = acc + ex[0][...]


def _epi_sigmoid(acc, i, ex, outs):
    outs[0][...] = jax.nn.sigmoid(acc)


def _epi_relu2(acc, i, ex, outs):
    r = jnp.maximum(acc, 0.0)
    outs[0][...] = (r * r).astype(outs[0].dtype)


def _epi_gate(acc, i, ex, outs):
    outs[0][...] = ex[0][...] * acc


def _epi_gate_add(acc, i, ex, outs):
    outs[0][...] = (ex[0][...] + ex[1][...] * acc).astype(outs[0].dtype)


def _make_epi_residual(tm, seq, n_batch):
    n_p = seq * n_batch

    def epi(acc, i, ex, outs):
        xp_ref, xs_ref, gp_ref, gs_ref = ex
        grow = i * tm + lax.broadcasted_iota(jnp.int32, acc.shape, 0)
        gate = gp_ref[n_batch - 1]
        for b in range(n_batch - 2, -1, -1):
            gate = jnp.where(grow < (b + 1) * seq, gp_ref[b], gate)
        pad = jnp.zeros((tm - DEC_ROWS, acc.shape[1]), F32)
        is_sample = grow >= n_p
        gate = jnp.where(is_sample, jnp.concatenate([pad, gs_ref[...]], axis=0), gate)
        x = jnp.where(is_sample, jnp.concatenate([pad, xs_ref[...]], axis=0), xp_ref[...])
        outs[0][...] = x + gate * acc

    return epi


def _mm_kernel(a_ref, w_ref, *refs, n_ex, epi, shift, w_is_t):
    if shift:
        w = jnp.concatenate([w_ref[shift:, :], refs[0][...]], axis=0)
        refs = refs[1:]
    else:
        w = w_ref[...]
    dims = (((1,), (1,)), ((), ())) if w_is_t else (((1,), (0,)), ((), ()))
    acc = lax.dot_general(a_ref[...], w.astype(BF16), dims, preferred_element_type=F32)
    epi(acc, pl.program_id(0), refs[:n_ex], refs[n_ex:])


def _mm(a, w, *, n_cols, col_off=0, shift=0, w_is_t=False, n_i, tn, out_dtype, epi, extras=(), extra_specs=(),
        name):
    m, kdim = a.shape
    tm = m // n_i
    assert tm * n_i == m and col_off % tn == 0 and n_cols % tn == 0 and (not shift or w_is_t)
    off = col_off // tn
    if w_is_t:
        w_spec = pl.BlockSpec((tn, kdim), lambda i, j: (j + off, 0))
    else:
        w_spec = pl.BlockSpec((kdim, tn), lambda i, j: (0, j + off))
    in_specs = [pl.BlockSpec((tm, kdim), lambda i, j: (i, 0)), w_spec]
    args = [a, w]
    if shift:
        assert shift % 8 == 0 and tn % shift == 0
        per = tn // shift
        in_specs.append(pl.BlockSpec((shift, kdim), lambda i, j: ((j + off + 1) * per, 0)))
        args.append(w)
    kern = functools.partial(_mm_kernel, n_ex=len(extras), epi=epi, shift=shift, w_is_t=w_is_t)
    return pl.pallas_call(
        kern, grid=(n_i, n_cols // tn), in_specs=in_specs + list(extra_specs),
        out_specs=pl.BlockSpec((tm, tn), lambda i, j: (i, j)),
        out_shape=jax.ShapeDtypeStruct((m, n_cols), out_dtype),
        compiler_params=_params(2), name=name)(*args, *extras)


def _ffn_down_kernel(a_ref, w_ref, x_ref, gp_ref, gs_ref, yp_ref, ys_ref, *, nk, tiles):
    i, k = pl.program_id(0), pl.program_id(2)
    @pl.when(k == 0)
    def _():
        yp_ref[...] = jnp.zeros(yp_ref.shape, F32)

    yp_ref[...] += jnp.dot(a_ref[...], w_ref[...].astype(BF16), preferred_element_type=F32)

    @pl.when(k == nk - 1)
    def _():
        def write(segs):
            for r0, r1, kind, b in segs:
                if kind == "s":
                    ys_ref[...] = x_ref[r0:r1, :] + gs_ref[...] * yp_ref[r0:r1, :]
            for r0, r1, kind, b in segs:
                if kind == "p":
                    yp_ref[r0:r1, :] = x_ref[r0:r1, :] + gp_ref[b] * yp_ref[r0:r1, :]

        _per_tile(i, tiles, write)


def _ffn_down(u, w, x, gate_p, gate_s, *, seq, n_batch, tn, tk):
    m, kdim = u.shape
    n_i = DOWN_ROW_TILES
    tm, nk, n_j = m // n_i, kdim // tk, D_MODEL // tn
    n_p = seq * n_batch
    tiles = _row_segments(tm, n_i, seq, n_batch, m - n_p)
    kern = functools.partial(_ffn_down_kernel, nk=nk, tiles=tiles)
    return pl.pallas_call(
        kern, grid=(n_i, n_j, nk),
        in_specs=[pl.BlockSpec((tm, tk), lambda i, j, k: (i, k)),
                  pl.BlockSpec((tk, tn), lambda i, j, k: (k, j)),
                  pl.BlockSpec((tm, tn), lambda i, j, k: (i, j)),
                  pl.BlockSpec((n_batch, 1, tn), lambda i, j, k: (0, 0, j)),
                  pl.BlockSpec((DEC_ROWS, tn), lambda i, j, k: (0, j))],
        out_specs=[pl.BlockSpec((tm, tn), lambda i, j, k: (i, j)),
                   pl.BlockSpec((DEC_ROWS, tn), lambda i, j, k: (0, jnp.where(i == n_i - 1, j, 0)))],
        out_shape=[jax.ShapeDtypeStruct((n_p, D_MODEL), F32), jax.ShapeDtypeStruct((m - n_p, D_MODEL), F32)],
        compiler_params=_params(3), name="ffn_down")(u, w, x, gate_p, gate_s)


def _place_kernel(big_ref, small_ref, o_ref):
    o_ref[...] = small_ref[...]


def _place_rows(big, small, row_block, name):
    return pl.pallas_call(
        _place_kernel, grid=(1,),
        in_specs=[pl.BlockSpec(memory_space=pl.ANY), pl.BlockSpec(small.shape, lambda i: (0, 0))],
        out_specs=pl.BlockSpec(small.shape, lambda i: (row_block, 0)),
        out_shape=jax.ShapeDtypeStruct(big.shape, big.dtype),
        input_output_aliases={0: 0}, compiler_params=_params(1), name=name)(big, small)


NORM_ROWS = 512


def _norm_mod_kernel(xp_ref, xs_ref, g_ref, scp_ref, shp_ref, scs_ref, shs_ref, o_ref, *, n_i):
    i = pl.program_id(0)

    def apply(x, sc, sh):
        y = x * lax.rsqrt(jnp.mean(x * x, axis=-1, keepdims=True) + EPS) * g_ref[...]
        return (y * (1.0 + sc) + sh).astype(o_ref.dtype)

    @pl.when(i < n_i - 1)
    def _():
        o_ref[...] = apply(xp_ref[...], scp_ref[0], shp_ref[0])

    @pl.when(i == n_i - 1)
    def _():
        o_ref[:DEC_ROWS, :] = apply(xs_ref[...], scs_ref[...], shs_ref[...])


def _norm_mod(x_p, x_s, s_block, g, sc_p, sh_p, sc_s, sh_s, *, n_p, rows_per_batch, name):
    d = x_p.shape[1]
    tm = NORM_ROWS
    n_i = n_p // tm + 1
    tpb = rows_per_batch // tm
    nb = sc_p.shape[0]
    pspec = pl.BlockSpec((1, 1, d), lambda i: (jnp.minimum(i // tpb, nb - 1), 0, 0))
    sspec = pl.BlockSpec((DEC_ROWS, d), lambda i: (0, 0))
    return pl.pallas_call(
        functools.partial(_norm_mod_kernel, n_i=n_i), grid=(n_i,),
        in_specs=[pl.BlockSpec((tm, d), lambda i: (jnp.minimum(i, n_i - 2), 0)),
                  pl.BlockSpec((DEC_ROWS, d), lambda i: (s_block, 0)),
                  pl.BlockSpec((1, d), lambda i: (0, 0)), pspec, pspec, sspec, sspec],
        out_specs=pl.BlockSpec((tm, d), lambda i: (i, 0)),
        out_shape=jax.ShapeDtypeStruct((n_p + DEC_ROWS, d), BF16),
        compiler_params=_params(1), name=name)(x_p, x_s, g, sc_p, sh_p, sc_s, sh_s)


def _norm_rope_kernel(x_ref, c_ref, s_ref, g_ref, o_ref, *, n_heads):
    c, s, g = c_ref[...], s_ref[...], g_ref[...]
    for h in range(n_heads):
        sl = slice(h * HEAD_DIM, (h + 1) * HEAD_DIM)
        o_ref[:, sl] = _norm_rope(x_ref[:, sl], g, c, s)


def _norm_rope_rows(x, row_block, c, s, g, *, n_heads, name):
    w = n_heads * HEAD_DIM
    return pl.pallas_call(
        functools.partial(_norm_rope_kernel, n_heads=n_heads), grid=(1,),
        in_specs=[pl.BlockSpec((DEC_ROWS, w), lambda i: (row_block, 0)),
                  pl.BlockSpec((DEC_ROWS, HEAD_DIM), lambda i: (0, 0)),
                  pl.BlockSpec((DEC_ROWS, HEAD_DIM), lambda i: (0, 0)),
                  pl.BlockSpec((1, HEAD_DIM), lambda i: (0, 0))],
        out_specs=pl.BlockSpec((DEC_ROWS, w), lambda i: (0, 0)),
        out_shape=jax.ShapeDtypeStruct((DEC_ROWS, w), F32),
        compiler_params=_params(1), name=name)(x, c, s, g)


def _k_prep_kernel(k_ref, v_ref, c_ref, s_ref, g_ref, ko_ref, ka_ref, va_ref, km_ref, *, n_blk):
    blk = pl.program_id(0) % n_blk
    c, s, g = c_ref[...], s_ref[...], g_ref[...]
    lane = lax.broadcasted_iota(jnp.int32, (MOBA_BLOCK, HEAD_DIM), 1)
    block_tag = jnp.where(lane == blk, 1.0, 0.0).astype(BF16)
    ones_col = jnp.where(lane == 0, 1.0, 0.0).astype(BF16)
    for h in range(KV_HEADS):
        sl = slice(h * HEAD_DIM, (h + 1) * HEAD_DIM)
        lo, mid, hi = 2 * h * HEAD_DIM, (2 * h + 1) * HEAD_DIM, (2 * h + 2) * HEAD_DIM
        kr = _norm_rope(k_ref[:, sl], g, c, s)
        ko_ref[:, sl] = kr
        ka_ref[:, lo:mid] = kr.astype(BF16)
        ka_ref[:, mid:hi] = block_tag
        va_ref[:, lo:mid] = v_ref[:, sl].astype(BF16)
        va_ref[:, mid:hi] = ones_col
        km_ref[0, :, sl] = jnp.mean(kr, axis=0, keepdims=True)


def _k_prep(kv, c, s, g, *, n_rows, n_blk):
    nt = n_rows // MOBA_BLOCK
    return pl.pallas_call(
        functools.partial(_k_prep_kernel, n_blk=n_blk), grid=(nt,),
        in_specs=[pl.BlockSpec((MOBA_BLOCK, KV_DIM), lambda t: (t, 0)),
                  pl.BlockSpec((MOBA_BLOCK, KV_DIM), lambda t: (t, 1)),
                  pl.BlockSpec((MOBA_BLOCK, HEAD_DIM), lambda t: (t % n_blk, 0)),
                  pl.BlockSpec((MOBA_BLOCK, HEAD_DIM), lambda t: (t % n_blk, 0)),
                  pl.BlockSpec((1, HEAD_DIM), lambda t: (0, 0))],
        out_specs=[pl.BlockSpec((MOBA_BLOCK, KV_DIM), lambda t: (t, 0)),
                   pl.BlockSpec((MOBA_BLOCK, 2 * KV_DIM), lambda t: (t, 0)),
                   pl.BlockSpec((MOBA_BLOCK, 2 * KV_DIM), lambda t: (t, 0)),
                   pl.BlockSpec((1, 1, KV_DIM), lambda t: (t, 0, 0))],
        out_shape=[jax.ShapeDtypeStruct((n_rows, KV_DIM), F32),
                   jax.ShapeDtypeStruct((n_rows, 2 * KV_DIM), BF16),
                   jax.ShapeDtypeStruct((n_rows, 2 * KV_DIM), BF16),
                   jax.ShapeDtypeStruct((nt, 1, KV_DIM), F32)],
        compiler_params=_params(1), name="k_prep")(kv, kv, c, s, g)


def _q_prep_kernel(q_ref, c_ref, s_ref, g_ref, km_ref, o_ref, *, n_blk):
    own = pl.program_id(0) % n_blk
    c, s, g = c_ref[...], s_ref[...], g_ref[...]
    n_idx = lax.broadcasted_iota(jnp.int32, (n_blk, MOBA_BLOCK), 0)
    lane = lax.broadcasted_iota(jnp.int32, (MOBA_BLOCK, HEAD_DIM), 1)
    past = n_idx < own
    km = km_ref[0]
    pad = jnp.zeros((HEAD_DIM - n_blk, MOBA_BLOCK), F32)
    for h in range(N_HEADS):
        kv = h // KV_REP
        qh = _norm_rope(q_ref[:, h * HEAD_DIM:(h + 1) * HEAD_DIM], g, c, s)
        gate = _dot_hp(km[:, kv * HEAD_DIM:(kv + 1) * HEAD_DIM], qh, (((1,), (1,)), ((), ())))
        gate = jnp.where(past, gate, -jnp.inf)
        sel = (_topk_select(gate, n_idx, n_blk) & past) | (n_idx == own)
        sel_t = jnp.concatenate([jnp.where(sel, 1.0, 0.0), pad], axis=0).T
        bias = jnp.where(lane < n_blk, (sel_t - 1.0) * (-MASK_SCORE), 0.0)
        o_ref[:, 2 * h * HEAD_DIM:(2 * h + 1) * HEAD_DIM] = (qh * ATTN_SCALE).astype(BF16)
        o_ref[:, (2 * h + 1) * HEAD_DIM:(2 * h + 2) * HEAD_DIM] = bias.astype(BF16)


def _q_prep(q, c, s, g, kmean, *, n_rows, n_blk):
    nt = n_rows // MOBA_BLOCK
    return pl.pallas_call(
        functools.partial(_q_prep_kernel, n_blk=n_blk), grid=(nt,),
        in_specs=[pl.BlockSpec((MOBA_BLOCK, Q_DIM), lambda t: (t, 0)),
                  pl.BlockSpec((MOBA_BLOCK, HEAD_DIM), lambda t: (t % n_blk, 0)),
                  pl.BlockSpec((MOBA_BLOCK, HEAD_DIM), lambda t: (t % n_blk, 0)),
                  pl.BlockSpec((1, HEAD_DIM), lambda t: (0, 0)),
                  pl.BlockSpec((1, n_blk, KV_DIM), lambda t: (t // n_blk, 0, 0))],
        out_specs=pl.BlockSpec((MOBA_BLOCK, 2 * Q_DIM), lambda t: (t, 0)),
        out_shape=jax.ShapeDtypeStruct((n_rows, 2 * Q_DIM), BF16),
        compiler_params=_params(1), name="q_prep")(q, c, s, g, kmean)


KEY_GROUP = 4 * MOBA_BLOCK


def _attn_p_kernel(q_ref, k_ref, v_ref, o_ref, m_ref, acc_ref):
    i = pl.program_id(2)
    rows = KV_REP * MOBA_BLOCK
    aug = 2 * HEAD_DIM
    qs = jnp.concatenate([q_ref[:, r * aug:(r + 1) * aug] for r in range(KV_REP)], axis=0)
    dims = (((1,), (1,)), ((), ()))
    blocks_per_group = KEY_GROUP // MOBA_BLOCK
    n_full = i // blocks_per_group
    group_start = pl.multiple_of(n_full * KEY_GROUP, KEY_GROUP)

    def own_group(n_blocks):
        width = n_blocks * MOBA_BLOCK
        s = lax.dot_general(qs, k_ref[pl.ds(group_start, width), :], dims, preferred_element_type=F32)
        qpos = (n_blocks - 1) * MOBA_BLOCK + (lax.broadcasted_iota(jnp.int32, (rows, width), 0) & (MOBA_BLOCK - 1))
        s = jnp.where(lax.broadcasted_iota(jnp.int32, (rows, width), 1) <= qpos, s, -jnp.inf)
        m = jnp.max(s, axis=1, keepdims=True)
        m_ref[...] = jnp.broadcast_to(m, (rows, LANES))
        acc_ref[...] = jnp.dot(jnp.exp(s - m).astype(BF16), v_ref[pl.ds(group_start, width), :],
                               preferred_element_type=F32)

    for n_blocks in range(1, blocks_per_group + 1):
        pl.when(i % blocks_per_group == n_blocks - 1)(functools.partial(own_group, n_blocks))

    def body(gi, carry):
        m, acc = carry
        st = pl.multiple_of(gi * KEY_GROUP, KEY_GROUP)
        s = lax.dot_general(qs, k_ref[pl.ds(st, KEY_GROUP), :], dims, preferred_element_type=F32)
        m_new = jnp.maximum(m, jnp.max(s, axis=1, keepdims=True))
        p = jnp.exp(s - jnp.tile(m_new, (1, KEY_GROUP // LANES)))
        pv = jnp.dot(p.astype(BF16), v_ref[pl.ds(st, KEY_GROUP), :], preferred_element_type=F32)
        return m_new, acc * jnp.tile(jnp.exp(m - m_new), (1, aug // LANES)) + pv

    m, acc = lax.fori_loop(0, n_full, body, (m_ref[...], acc_ref[...]))
    o = acc[:, :HEAD_DIM] / acc[:, HEAD_DIM:HEAD_DIM + 1]
    for r in range(KV_REP):
        o_ref[:, r * HEAD_DIM:(r + 1) * HEAD_DIM] = o[r * MOBA_BLOCK:(r + 1) * MOBA_BLOCK].astype(o_ref.dtype)


def _attn_prompt(q_aug, k_aug, v_aug, *, n_batch, seq, out_rows):
    n_blk = seq // MOBA_BLOCK
    aug = 2 * HEAD_DIM
    return pl.pallas_call(
        _attn_p_kernel, grid=(n_batch, KV_HEADS, n_blk),
        in_specs=[pl.BlockSpec((MOBA_BLOCK, KV_REP * aug), lambda b, g, i: (b * n_blk + i, g)),
                  pl.BlockSpec((seq, aug), lambda b, g, i: (b, g)),
                  pl.BlockSpec((seq, aug), lambda b, g, i: (b, g))],
        out_specs=pl.BlockSpec((MOBA_BLOCK, KV_REP * HEAD_DIM), lambda b, g, i: (b * n_blk + i, g)),
        out_shape=jax.ShapeDtypeStruct((out_rows, Q_DIM), BF16),
        scratch_shapes=[pltpu.VMEM((KV_REP * MOBA_BLOCK, LANES), F32), pltpu.VMEM((KV_REP * MOBA_BLOCK, aug), F32)],
        compiler_params=_params(3), name="attn_prompt")(q_aug, k_aug, v_aug)


GROUP_ROWS = 16
PAGES_PER_STEP = 8


def _diag_blocks(full):
    return jnp.concatenate([full[g * GROUP_ROWS:(g + 1) * GROUP_ROWS, g * HEAD_DIM:(g + 1) * HEAD_DIM]
                            for g in range(KV_HEADS)], axis=0)


def _page_tile(ref):
    return jnp.concatenate([ref[0, pl.ds(g, PAGE_SIZE, stride=KV_HEADS), :] for g in range(KV_HEADS)], axis=1)


def _attn_s_kernel(pt_ref, *refs, n_blk, dec_seq):
    n_pg = PAGES_PER_STEP
    k_refs, v_refs = refs[:n_pg], refs[n_pg:2 * n_pg]
    qb_ref, qf_ref, ko_ref, vo_ref, o_ref, ob_ref, m_ref, l_ref, ks_ref = refs[2 * n_pg:]
    step = pl.program_id(1)
    pages_per_block = MOBA_BLOCK // PAGE_SIZE
    blocks_per_step = n_pg // pages_per_block
    qb = qb_ref[0]
    for u in range(blocks_per_step):
        jb = step * blocks_per_step + u
        pages = range(u * pages_per_block, (u + 1) * pages_per_block)
        kblk = jnp.concatenate([_page_tile(k_refs[p]) for p in pages], axis=0)
        vblk = jnp.concatenate([_page_tile(v_refs[p]) for p in pages], axis=0).astype(BF16)
        ks_ref[pl.ds(jb, 1), :] = jnp.sum(kblk, axis=0, keepdims=True)
        st = jnp.dot(kblk.astype(BF16), qb, preferred_element_type=F32)
        mb = jnp.max(st, axis=0, keepdims=True)
        p = jnp.exp(st - mb)
        m_ref[pl.ds(jb, 1), :] = mb
        l_ref[pl.ds(jb, 1), :] = jnp.sum(p, axis=0, keepdims=True)
        ob_ref[jb] = _diag_blocks(jnp.dot(p.T.astype(BF16), vblk, preferred_element_type=F32))

    @pl.when(step == n_blk // blocks_per_step - 1)
    def _():
        gate = _dot_hp(ks_ref[...] * (1.0 / MOBA_BLOCK), qf_ref[0], (((1,), (0,)), ((), ())))
        n_idx = lax.broadcasted_iota(jnp.int32, (n_blk, LANES), 0)
        sel = _topk_select(gate, n_idx, n_blk)
        m_all = m_ref[...]
        so = jnp.dot(ko_ref[0].astype(BF16), qb, preferred_element_type=F32)
        krow = lax.broadcasted_iota(jnp.int32, so.shape, 0)
        tq = (lax.broadcasted_iota(jnp.int32, so.shape, 1) >> 2) & (dec_seq - 1)
        so = jnp.where((krow <= tq) & (krow < dec_seq), so, -jnp.inf)
        m_tot = jnp.maximum(jnp.max(jnp.where(sel, m_all, -jnp.inf), axis=0, keepdims=True),
                            jnp.max(so, axis=0, keepdims=True))
        w = jnp.where(sel, jnp.exp(m_all - m_tot), 0.0)
        po = jnp.exp(so - m_tot)
        inv = 1.0 / (jnp.sum(w * l_ref[...], axis=0, keepdims=True) + jnp.sum(po, axis=0, keepdims=True))
        pad = jnp.zeros((LANES - n_blk - so.shape[0], LANES), F32)
        wt = jnp.concatenate([w * inv, po * inv, pad], axis=0).T
        acc = jnp.zeros((LANES, HEAD_DIM), F32)
        for b in range(n_blk):
            acc = acc + wt[:, b:b + 1] * ob_ref[b]
        vo = vo_ref[0]
        for t in range(dec_seq):
            vrow = jnp.concatenate(
                [jnp.broadcast_to(vo[t:t + 1, g * HEAD_DIM:(g + 1) * HEAD_DIM], (GROUP_ROWS, HEAD_DIM))
                 for g in range(KV_HEADS)], axis=0)
            acc = acc + wt[:, n_blk + t:n_blk + t + 1] * vrow
        o_ref[0] = acc


def _attn_sample(page_table, cache_k, cache_v, qbd, qbd_f32, k_own, v_own, *, dec_seq):
    n_seq, n_pages = page_table.shape
    n_blk = n_pages * PAGE_SIZE // MOBA_BLOCK
    page_rows = PAGE_SIZE * KV_HEADS
    n_pg = PAGES_PER_STEP
    assert n_pages % n_pg == 0
    pages = [pl.BlockSpec((1, page_rows, HEAD_DIM), lambda s, t, pt, p=p: (pt[s, t * n_pg + p], 0, 0))
             for p in range(n_pg)]
    per_seq = lambda r, c: pl.BlockSpec((1, r, c), lambda s, t, pt: (s, 0, 0))
    grid_spec = pltpu.PrefetchScalarGridSpec(
        num_scalar_prefetch=1, grid=(n_seq, n_pages // n_pg),
        in_specs=pages + pages + [per_seq(KV_DIM, LANES), per_seq(KV_DIM, LANES),
                                  per_seq(8, KV_DIM), per_seq(8, KV_DIM)],
        out_specs=per_seq(LANES, HEAD_DIM),
        scratch_shapes=[pltpu.VMEM((n_blk, LANES, HEAD_DIM), F32), pltpu.VMEM((n_blk, LANES), F32),
                        pltpu.VMEM((n_blk, LANES), F32), pltpu.VMEM((n_blk, KV_DIM), F32)])
    return pl.pallas_call(
        functools.partial(_attn_s_kernel, n_blk=n_blk, dec_seq=dec_seq), grid_spec=grid_spec,
        out_shape=jax.ShapeDtypeStruct((n_seq, LANES, HEAD_DIM), F32),
        compiler_params=_params(2), name="attn_sample")(
            page_table, *([cache_k] * n_pg), *([cache_v] * n_pg), qbd, qbd_f32, k_own, v_own)


CONV_PAD = 8
SHORT_CHUNK = 16


def _chunk_t(x, L):
    if L == LANES:
        return x.T
    return jnp.concatenate([x, jnp.zeros((LANES - L, LANES), F32)], axis=0).T[:, :L]


def _ssd_kernel(xbc_ref, z_ref, dt_ref, cprev_ref, h0_ref, wc_ref, bc_ref, dtb_ref, alog_ref, dx_ref, gs_ref,
                e2_ref, e1_ref, tril_ref, y_ref, hf_ref, cnew_ref, ht_ref, cbuf_ref, *, L, nc, n_valid, rows_in):
    c = pl.program_id(1)
    gw = SSM_REP * SSM_HEAD_DIM

    @pl.when(c == 0)
    def _():
        cbuf_ref[0:CONV_PAD, :] = cprev_ref[0]
        if rows_in < L:
            cbuf_ref[CONV_PAD + rows_in:, :] = jnp.zeros((L - rows_in, CONV_DIM), F32)
        for g in range(SSM_GROUPS):
            ht_ref[g] = h0_ref[0, g * SSM_REP:(g + 1) * SSM_REP].reshape(gw, SSM_STATE).T

    cbuf_ref[CONV_PAD:CONV_PAD + rows_in, :] = xbc_ref[...]
    first = CONV_PAD - (CONV_W - 1)
    conv = bc_ref[...]
    for j in range(CONV_W):
        conv = conv + cbuf_ref[first + j:first + j + L, :] * wc_ref[j:j + 1, :]
    cnew_ref[0] = cbuf_ref[n_valid:CONV_PAD + n_valid, :]
    cbuf_ref[first:CONV_PAD, :] = cbuf_ref[first + L:CONV_PAD + L, :]

    act = conv * jax.nn.sigmoid(conv)
    xs = act[:, :SSM_INNER]
    if rows_in < L:
        zpad = jnp.zeros((L - rows_in, SSM_INNER), F32)
        z = jnp.concatenate([z_ref[...], zpad], axis=0)
        dt_raw = jnp.concatenate([dt_ref[...], jnp.zeros((L - rows_in, LANES), F32)], axis=0)
    else:
        z, dt_raw = z_ref[...], dt_ref[...]

    row = lax.broadcasted_iota(jnp.int32, (L, LANES), 0)
    lane = lax.broadcasted_iota(jnp.int32, (L, LANES), 1)
    dt = jax.nn.softplus(dt_raw + dtb_ref[...])
    dt = jnp.where((row < n_valid) & (lane < SSM_HEADS), dt, 0.0)
    cum = _dot3_lhs_exact(tril_ref[...], dt * (-jnp.exp(alog_ref[...])))
    cum_t, dt_t = _chunk_t(cum, L), _chunk_t(dt, L)
    causal = lax.broadcasted_iota(jnp.int32, (L, L), 1) <= lax.broadcasted_iota(jnp.int32, (L, L), 0)

    outs = []
    for g in range(SSM_GROUPS):
        sl = slice(g * gw, (g + 1) * gw)
        bg = act[:, SSM_INNER + g * SSM_STATE:SSM_INNER + (g + 1) * SSM_STATE]
        cg = act[:, SSM_INNER + (SSM_GROUPS + g) * SSM_STATE:SSM_INNER + (SSM_GROUPS + g + 1) * SSM_STATE]
        bg_b, cg_b = bg.astype(BF16), cg.astype(BF16)
        cb = lax.dot_general(cg_b, bg_b, (((1,), (1,)), ((), ())), preferred_element_type=F32)
        cum_x2 = _dot3_rhs_exact(cum, e2_ref[:, g * SSM_REP * LANES:(g + 1) * SSM_REP * LANES])
        cum_x = _dot3_rhs_exact(cum, e1_ref[:, sl])
        dt_x = _dot3_rhs_exact(dt, e1_ref[:, sl])
        ht = ht_ref[g]
        y_inter = jnp.dot(cg_b, ht.astype(BF16), preferred_element_type=F32)
        parts = []
        for r in range(SSM_REP):
            h = g * SSM_REP + r
            seg = cum_x2[:, r * LANES:r * LANES + L] - cum_t[h:h + 1, :]
            w = cb * jnp.exp(jnp.where(causal, seg, -jnp.inf)) * dt_t[h:h + 1, :]
            xh = xs[:, h * SSM_HEAD_DIM:(h + 1) * SSM_HEAD_DIM].astype(BF16)
            parts.append(jnp.dot(w.astype(BF16), xh, preferred_element_type=F32))
        xg = xs[:, sl]
        yg = jnp.concatenate(parts, axis=1) + y_inter * jnp.exp(cum_x) + dx_ref[:, sl] * xg
        last = cum_x[L - 1:L, :]
        xw = (xg * (jnp.exp(last - cum_x) * dt_x)).astype(BF16)
        ht_ref[g] = ht * jnp.exp(last) + jnp.dot(_chunk_t(bg, L).astype(BF16), xw, preferred_element_type=F32)
        zg = z[:, sl]
        yg = yg * (zg * jax.nn.sigmoid(zg))
        yg = yg * lax.rsqrt(jnp.mean(yg * yg, axis=-1, keepdims=True) + EPS) * gs_ref[:, sl]
        outs.append(yg[:rows_in].astype(y_ref.dtype))
    y_ref[...] = jnp.concatenate(outs, axis=1)

    @pl.when(c == nc - 1)
    def _():
        for g in range(SSM_GROUPS):
            hf_ref[0, g * SSM_REP:(g + 1) * SSM_REP] = ht_ref[g].T.reshape(SSM_REP, SSM_HEAD_DIM, SSM_STATE)


def _ssd(xbc, z, dt, conv_prev, h0, w_conv, b_conv, dt_bias, a_log, d_skip, g_ssm, *, n_seq, nc, rows_in, n_valid,
         out_rows, chunk):
    L = chunk
    head_of = jnp.arange(SSM_HEADS * LANES, dtype=jnp.int32) // LANES
    e2 = (jnp.arange(LANES, dtype=jnp.int32)[:, None] == head_of[None, :]).astype(BF16)
    chan_head = jnp.arange(SSM_INNER, dtype=jnp.int32) // SSM_HEAD_DIM
    e1 = (jnp.arange(LANES, dtype=jnp.int32)[:, None] == chan_head[None, :]).astype(BF16)
    tril = (jnp.arange(L)[:, None] >= jnp.arange(L)[None, :]).astype(BF16)
    pad_l = lambda v: jnp.pad(v.reshape(1, -1), ((0, 0), (0, LANES - v.shape[0])))
    dx = jnp.repeat(d_skip, SSM_HEAD_DIM).reshape(1, SSM_INNER)
    const = lambda shape: pl.BlockSpec(shape, lambda b, c: (0,) * len(shape))
    kern = functools.partial(_ssd_kernel, L=L, nc=nc, n_valid=n_valid, rows_in=rows_in)
    return pl.pallas_call(
        kern, grid=(n_seq, nc),
        in_specs=[pl.BlockSpec((rows_in, CONV_DIM), lambda b, c: (b * nc + c, 0)),
                  pl.BlockSpec((rows_in, SSM_INNER), lambda b, c: (b * nc + c, 0)),
                  pl.BlockSpec((rows_in, LANES), lambda b, c: (b * nc + c, 0)),
                  pl.BlockSpec((1, CONV_PAD, CONV_DIM), lambda b, c: (b, 0, 0)),
                  pl.BlockSpec((1, SSM_HEADS, SSM_HEAD_DIM, SSM_STATE), lambda b, c: (b, 0, 0, 0)),
                  const((CONV_W, CONV_DIM)), const((1, CONV_DIM)), const((1, LANES)), const((1, LANES)),
                  const((1, SSM_INNER)), const((1, SSM_INNER)),
                  const((LANES, SSM_HEADS * LANES)), const((LANES, SSM_INNER)), const((L, L))],
        out_specs=[pl.BlockSpec((rows_in, SSM_INNER), lambda b, c: (b * nc + c, 0)),
                   pl.BlockSpec((1, SSM_HEADS, SSM_HEAD_DIM, SSM_STATE), lambda b, c: (b, 0, 0, 0)),
                   pl.BlockSpec((1, CONV_PAD, CONV_DIM), lambda b, c: (b, 0, 0))],
        out_shape=[jax.ShapeDtypeStruct((out_rows, SSM_INNER), BF16),
                   jax.ShapeDtypeStruct((n_seq, SSM_HEADS, SSM_HEAD_DIM, SSM_STATE), F32),
                   jax.ShapeDtypeStruct((n_seq, CONV_PAD, CONV_DIM), F32)],
        scratch_shapes=[pltpu.VMEM((SSM_GROUPS, SSM_STATE, SSM_REP * SSM_HEAD_DIM), F32),
                        pltpu.VMEM((CONV_PAD + L, CONV_DIM), F32)],
        compiler_params=_params(2), name="ssd")(
            xbc, z, dt, conv_prev, h0, w_conv, b_conv.reshape(1, -1), pad_l(dt_bias), pad_l(a_log), dx,
            g_ssm.reshape(1, -1), e2, e1, tril)


def kernel(x_prompt, x_sample, cache_k, cache_v, state_ssm, state_conv, page_table, c_prompt, c_sample, w_ada, b_ada, g_norm1, w_in, q_norm_g, k_norm_g, w_conv, b_conv, dt_bias, a_log, d_skip, g_ssm, w_attn_o, w_ssm_o, w_out, g_norm2, w_up, w_down):
    n_batch, seq, d = x_prompt.shape
    n_dec, dec_seq, _ = x_sample.shape
    n_p, n_s = n_batch * seq, n_dec * dec_seq
    n_rows = n_p + n_s
    assert n_s == DEC_ROWS and d == D_MODEL and seq % KEY_GROUP == 0 and n_p % DEC_ROWS == 0
    n_pool = cache_k.shape[0]
    past_len = page_table.shape[1] * PAGE_SIZE
    n_blk = seq // MOBA_BLOCK
    sample_block = n_p // DEC_ROWS
    xp2, xs2 = x_prompt.reshape(n_p, D_MODEL), x_sample.reshape(n_s, D_MODEL)
    row1 = lambda v: v.reshape(1, -1)

    c_all = jnp.concatenate([c_prompt, c_sample], axis=0)
    c_pad = jnp.pad(c_all, ((0, -c_all.shape[0] % 16), (0, 0))).astype(BF16)
    tn = PROJ_TN
    mod = _mm(c_pad, w_ada, n_cols=6 * D_MODEL, n_i=1, tn=tn, out_dtype=F32, epi=_epi_bias,
              extras=(row1(b_ada),), extra_specs=[pl.BlockSpec((1, tn), lambda i, j: (0, j))], name="ada_mod")
    mod = mod[:n_batch + n_dec].reshape(n_batch + n_dec, 6, D_MODEL)
    sh1, sc1, gt1, sh2, sc2, gt2 = (
        (mod[:n_batch, i].reshape(n_batch, 1, D_MODEL), jnp.repeat(mod[n_batch:, i], dec_seq, axis=0))
        for i in range(6))

    h = _norm_mod(xp2, xs2, 0, row1(g_norm1), sc1[0], sh1[0], sc1[1], sh1[1], n_p=n_p, rows_per_batch=seq,
                  name="norm_mod1")
    w_in_t = w_in.T
    proj = functools.partial(_mm, h, w_in_t, w_is_t=True, n_i=ROW_TILES, out_dtype=F32)
    q = proj(n_cols=Q_DIM, col_off=0, tn=tn, epi=_epi_store, name="in_q")
    kv = proj(n_cols=2 * KV_DIM, col_off=Q_DIM, tn=tn, epi=_epi_store, name="in_kv")
    z = proj(n_cols=SSM_INNER, col_off=Q_DIM + 2 * KV_DIM, tn=tn, epi=_epi_store, name="in_z")
    xbc = proj(n_cols=CONV_DIM, col_off=Q_DIM + 2 * KV_DIM + SSM_INNER, tn=tn, epi=_epi_store, name="in_xbc")
    dt_off = Q_DIM + 2 * KV_DIM + SSM_INNER + CONV_DIM
    dt = proj(n_cols=LANES, col_off=dt_off, tn=LANES, epi=_epi_store, name="in_dt")
    gates = proj(n_cols=2 * D_MODEL, col_off=dt_off, shift=SSM_HEADS, tn=tn, epi=_epi_sigmoid, name="in_gates")

    cp, sp = _rope_tables(jnp.arange(seq, dtype=jnp.int32))
    k_p, k_aug, v_aug, kmean = _k_prep(kv, cp, sp, row1(k_norm_g), n_rows=n_p, n_blk=n_blk)
    q_aug = _q_prep(q, cp, sp, row1(q_norm_g), kmean.reshape(n_batch, n_blk, KV_DIM), n_rows=n_p, n_blk=n_blk)
    attn = _attn_prompt(q_aug, k_aug, v_aug, n_batch=n_batch, seq=seq, out_rows=n_rows)

    pos_s = past_len + (jnp.arange(n_s, dtype=jnp.int32) % dec_seq)
    cs, ss = _rope_tables(pos_s)
    q_s = _norm_rope_rows(q, sample_block, cs, ss, row1(q_norm_g), n_heads=N_HEADS, name="q_sample")
    k_s = _norm_rope_rows(kv, sample_block, cs, ss, row1(k_norm_g), n_heads=KV_HEADS, name="k_sample")
    v_s = kv[n_p:, KV_DIM:]
    q_t = q_s.reshape(n_dec, dec_seq, KV_HEADS, KV_REP, HEAD_DIM).transpose(0, 2, 4, 1, 3)
    q_t = q_t.reshape(n_dec, KV_HEADS, HEAD_DIM, GROUP_ROWS)
    eye = jnp.eye(KV_HEADS, dtype=F32)
    qbd_f32 = (q_t[:, :, :, None, :] * eye[None, :, None, :, None]).reshape(n_dec, KV_DIM, LANES)
    qbd = (qbd_f32 * ATTN_SCALE).astype(BF16)
    own_pad = lambda v: jnp.pad(v.reshape(n_dec, dec_seq, KV_DIM), ((0, 0), (0, 8 - dec_seq), (0, 0)))
    pool_view = lambda cache: cache.reshape(n_pool, PAGE_SIZE * KV_HEADS, HEAD_DIM)
    o_s = _attn_sample(page_table, pool_view(cache_k), pool_view(cache_v), qbd, qbd_f32, own_pad(k_s), own_pad(v_s),
                       dec_seq=dec_seq)
    attn_s = o_s.reshape(n_dec, KV_HEADS, dec_seq, KV_REP, HEAD_DIM).transpose(0, 2, 1, 3, 4)
    attn = _place_rows(attn, attn_s.reshape(n_s, Q_DIM).astype(BF16), sample_block, "attn_place")

    ssd = functools.partial(_ssd, w_conv=w_conv, b_conv=b_conv, dt_bias=dt_bias, a_log=a_log, d_skip=d_skip,
                            g_ssm=g_ssm)
    conv_pad = lambda v: jnp.pad(v, ((0, 0), (CONV_PAD - (CONV_W - 1), 0), (0, 0)))
    zeros_h = jnp.zeros((n_batch, SSM_HEADS, SSM_HEAD_DIM, SSM_STATE), F32)
    zeros_c = jnp.zeros((n_batch, CONV_W - 1, CONV_DIM), F32)
    y_ssm, ssm_p, cnew_p = ssd(xbc, z, dt, conv_pad(zeros_c), zeros_h, n_seq=n_batch, nc=seq // SSM_CHUNK,
                               rows_in=SSM_CHUNK, n_valid=SSM_CHUNK, out_rows=n_rows, chunk=SSM_CHUNK)
    short = SHORT_CHUNK
    pad_s = lambda v: jnp.pad(v[n_p:].reshape(n_dec, dec_seq, -1), ((0, 0), (0, short - dec_seq), (0, 0))).reshape(
        n_dec * short, -1)
    y_s, ssm_s, cnew_s = ssd(pad_s(xbc), pad_s(z), pad_s(dt), conv_pad(state_conv), state_ssm, n_seq=n_dec, nc=1,
                             rows_in=short, n_valid=dec_seq, out_rows=n_dec * short, chunk=short)
    y_s = y_s.reshape(n_dec, short, SSM_INNER)[:, :dec_seq].reshape(n_s, SSM_INNER)
    y_ssm = _place_rows(y_ssm, y_s, sample_block, "ssm_place")

    n_i = GATED_ROW_TILES
    tm = n_rows // n_i
    tile = lambda off: pl.BlockSpec((tm, tn), lambda i, j: (i, j + off))
    mix_a = _mm(attn, w_attn_o, n_cols=D_MODEL, n_i=n_i, tn=tn, out_dtype=F32, epi=_epi_gate,
                extras=(gates,), extra_specs=[tile(0)], name="attn_proj")
    mixed = _mm(y_ssm, w_ssm_o, n_cols=D_MODEL, n_i=n_i, tn=tn, out_dtype=BF16, epi=_epi_gate_add,
                extras=(mix_a, gates), extra_specs=[tile(0), tile(D_MODEL // tn)], name="ssm_proj")
    assert n_rows - n_s == n_p and tm >= n_s
    res_specs = [tile(0), pl.BlockSpec((DEC_ROWS, tn), lambda i, j: (0, j)),
                 pl.BlockSpec((n_batch, 1, tn), lambda i, j: (0, 0, j)),
                 pl.BlockSpec((DEC_ROWS, tn), lambda i, j: (0, j))]
    x1 = _mm(mixed, w_out, n_cols=D_MODEL, n_i=n_i, tn=tn, out_dtype=F32, epi=_make_epi_residual(tm, seq, n_batch),
             extras=(xp2, xs2, gt1[0], gt1[1]), extra_specs=res_specs, name="out_proj")

    h2 = _norm_mod(x1, x1, sample_block, row1(g_norm2), sc2[0], sh2[0], sc2[1], sh2[1], n_p=n_p,
                   rows_per_batch=seq, name="norm_mod2")
    u = _mm(h2, w_up, n_cols=D_FF, n_i=ROW_TILES, tn=PROJ_TN, out_dtype=BF16, epi=_epi_relu2, name="ffn_up")
    y_p, y_s2 = _ffn_down(u, w_down, x1, gt2[0], gt2[1], seq=seq, n_batch=n_batch, tn=512, tk=2048)

    kv4 = lambda v, b, t: v.reshape(b, t, KV_HEADS, HEAD_DIM)
    first = CONV_PAD - (CONV_W - 1)
    return (y_p.reshape(n_batch, seq, D_MODEL), y_s2.reshape(n_dec, dec_seq, D_MODEL),
            kv4(k_p, n_batch, seq), kv4(kv[:n_p, KV_DIM:], n_batch, seq),
            ssm_p, cnew_p[:, first:],
            kv4(k_s, n_dec, dec_seq), kv4(v_s, n_dec, dec_seq),
            ssm_s, cnew_s[:, first:])
```

```python
import functools

import jax
import jax.numpy as jnp
from jax import lax
from jax.experimental import pallas as pl
from jax.experimental.pallas import tpu as pltpu

F32, BF16 = jnp.float32, jnp.bfloat16

D_MODEL = 4096
HEAD_DIM = 128
N_HEADS = 32
KV_HEADS = 8
KV_REP = 4
MOBA_BLOCK = 256
MOBA_TOPK = 3
ROPE_THETA = 10000.0
PAGE_SIZE = 128
SSM_HEADS = 64
SSM_HEAD_DIM = 64
SSM_STATE = 128
SSM_GROUPS = 8
SSM_REP = 8
SSM_INNER = 4096
SSM_CHUNK = 128
CONV_W = 4
CONV_DIM = 6144
D_FF = 16384
EPS = 1e-6
Q_DIM = N_HEADS * HEAD_DIM
KV_DIM = KV_HEADS * HEAD_DIM
ATTN_SCALE = HEAD_DIM ** -0.5
MASK_SCORE = -1e30
VMEM_LIMIT_BYTES = 60 * 1024 * 1024
LANES = 128
DEC_ROWS = 128
ROW_TILES = 5
PROJ_TN = 512
GATED_ROW_TILES = 8
DOWN_ROW_TILES = 4


def _params(n_axes):
    return pltpu.CompilerParams(dimension_semantics=("arbitrary",) * n_axes,
                                vmem_limit_bytes=VMEM_LIMIT_BYTES)


def _split3(x):
    hi = x.astype(BF16)
    r1 = x - hi.astype(F32)
    mid = r1.astype(BF16)
    lo = (r1 - mid.astype(F32)).astype(BF16)
    return hi, mid, lo


def _dot3_lhs_exact(w_exact, x):
    hi, mid, lo = _split3(x)
    f = functools.partial(jnp.dot, preferred_element_type=F32)
    return f(w_exact, lo) + f(w_exact, mid) + f(w_exact, hi)


def _dot3_rhs_exact(x, w_exact):
    hi, mid, lo = _split3(x)
    f = functools.partial(jnp.dot, preferred_element_type=F32)
    return f(lo, w_exact) + f(mid, w_exact) + f(hi, w_exact)


def _dot_hp(a, b, dims):
    a0, a1, a2 = _split3(a)
    b0, b1, b2 = _split3(b)
    f = functools.partial(lax.dot_general, dimension_numbers=dims, preferred_element_type=F32)
    return (f(a2, b0) + f(a1, b1) + f(a0, b2)) + (f(a1, b0) + f(a0, b1)) + f(a0, b0)


def _norm_rope(xh, g, c, s):
    xn = xh * lax.rsqrt(jnp.mean(xh * xh, axis=-1, keepdims=True) + EPS) * g
    return xn * c + pltpu.roll(xn, HEAD_DIM // 2, 1) * s


def _topk_select(gate, n_idx, n_blocks):
    cnt = jnp.zeros(gate.shape, F32)
    for m in range(n_blocks):
        rm = gate[m:m + 1, :]
        beats = jnp.where(rm > gate, 1.0, jnp.where((rm == gate) & (n_idx > m), 1.0, 0.0))
        cnt = cnt + beats
    return cnt < MOBA_TOPK


def _rope_tables(pos):
    half = HEAD_DIM // 2
    inv = ROPE_THETA ** (-jnp.arange(half, dtype=F32) / half)
    ang = pos.astype(F32)[:, None] * inv[None, :]
    c, s = jnp.cos(ang), jnp.sin(ang)
    return jnp.concatenate([c, c], axis=-1), jnp.concatenate([-s, s], axis=-1)


def _row_segments(tm, n_tiles, seq, n_batch, n_s):
    n_p = seq * n_batch
    tiles = []
    for t in range(n_tiles):
        lo, hi, cur = t * tm, (t + 1) * tm, []
        for b in range(n_batch):
            a, e = max(lo, b * seq), min(hi, (b + 1) * seq)
            if a < e:
                cur.append((a - lo, e - lo, "p", b))
        a, e = max(lo, n_p), min(hi, n_p + n_s)
        if a < e:
            assert (a, e) == (n_p, n_p + n_s) and (a - lo) % 8 == 0
            cur.append((a - lo, e - lo, "s", 0))
        tiles.append(cur)
    return tiles


def _per_tile(i, tiles, fn):
    for t, segs in enumerate(tiles):
        pl.when(i == t)(functools.partial(fn, segs))


def _epi_store(acc, i, ex, outs):
    outs[0][...] = acc.astype(outs[0].dtype)


def _epi_bias(acc, i, ex, outs):
    outs[0][...] = acc + ex[0][...]


def _epi_sigmoid(acc, i, ex, outs):
    outs[0][...] = jax.nn.sigmoid(acc)


def _epi_relu2(acc, i, ex, outs):
    r = jnp.maximum(acc, 0.0)
    outs[0][...] = (r * r).astype(outs[0].dtype)


def _epi_gate(acc, i, ex, outs):
    outs[0][...] = ex[0][...] * acc


def _epi_gate_add(acc, i, ex, outs):
    outs[0][...] = (ex[0][...] + ex[1][...] * acc).astype(outs[0].dtype)


def _make_epi_residual(tm, seq, n_batch):
    n_p = seq * n_batch

    def epi(acc, i, ex, outs):
        xp_ref, xs_ref, gp_ref, gs_ref = ex
        grow = i * tm + lax.broadcasted_iota(jnp.int32, acc.shape, 0)
        gate = gp_ref[n_batch - 1]
        for b in range(n_batch - 2, -1, -1):
            gate = jnp.where(grow < (b + 1) * seq, gp_ref[b], gate)
        pad = jnp.zeros((tm - DEC_ROWS, acc.shape[1]), F32)
        is_sample = grow >= n_p
        gate = jnp.where(is_sample, jnp.concatenate([pad, gs_ref[...]], axis=0), gate)
        x = jnp.where(is_sample, jnp.concatenate([pad, xs_ref[...]], axis=0), xp_ref[...])
        outs[0][...] = x + gate * acc

    return epi


def _mm_kernel(a_ref, w_ref, *refs, n_ex, epi, shift, w_is_t):
    if shift:
        w = jnp.concatenate([w_ref[shift:, :], refs[0][...]], axis=0)
        refs = refs[1:]
    else:
        w = w_ref[...]
    dims = (((1,), (1,)), ((), ())) if w_is_t else (((1,), (0,)), ((), ()))
    acc = lax.dot_general(a_ref[...], w.astype(BF16), dims, preferred_element_type=F32)
    epi(acc, pl.program_id(0), refs[:n_ex], refs[n_ex:])


def _mm(a, w, *, n_cols, col_off=0, shift=0, w_is_t=False, n_i, tn, out_dtype, epi, extras=(), extra_specs=(),
        name):
    m, kdim = a.shape
    tm = m // n_i
    assert tm * n_i == m and col_off % tn == 0 and n_cols % tn == 0 and (not shift or w_is_t)
    off = col_off // tn
    if w_is_t:
        w_spec = pl.BlockSpec((tn, kdim), lambda i, j: (j + off, 0))
    else:
        w_spec = pl.BlockSpec((kdim, tn), lambda i, j: (0, j + off))
    in_specs = [pl.BlockSpec((tm, kdim), lambda i, j: (i, 0)), w_spec]
    args = [a, w]
    if shift:
        assert shift % 8 == 0 and tn % shift == 0
        per = tn // shift
        in_specs.append(pl.BlockSpec((shift, kdim), lambda i, j: ((j + off + 1) * per, 0)))
        args.append(w)
    kern = functools.partial(_mm_kernel, n_ex=len(extras), epi=epi, shift=shift, w_is_t=w_is_t)
    return pl.pallas_call(
        kern, grid=(n_i, n_cols // tn), in_specs=in_specs + list(extra_specs),
        out_specs=pl.BlockSpec((tm, tn), lambda i, j: (i, j)),
        out_shape=jax.ShapeDtypeStruct((m, n_cols), out_dtype),
        compiler_params=_params(2), name=name)(*args, *extras)


def _ffn_down_kernel(a_ref, w_ref, x_ref, gp_ref, gs_ref, yp_ref, ys_ref, *, nk, tiles):
    i, k = pl.program_id(0), pl.program_id(2)
    @pl.when(k == 0)
    def _():
        yp_ref[...] = jnp.zeros(yp_ref.shape, F32)

    yp_ref[...] += jnp.dot(a_ref[...], w_ref[...].astype(BF16), preferred_element_type=F32)

    @pl.when(k == nk - 1)
    def _():
        def write(segs):
            for r0, r1, kind, b in segs:
                if kind == "s":
                    ys_ref[...] = x_ref[r0:r1, :] + gs_ref[...] * yp_ref[r0:r1, :]
            for r0, r1, kind, b in segs:
                if kind == "p":
                    yp_ref[r0:r1, :] = x_ref[r0:r1, :] + gp_ref[b] * yp_ref[r0:r1, :]

        _per_tile(i, tiles, write)


def _ffn_down(u, w, x, gate_p, gate_s, *, seq, n_batch, tn, tk):
    m, kdim = u.shape
    n_i = DOWN_ROW_TILES
    tm, nk, n_j = m // n_i, kdim // tk, D_MODEL // tn
    n_p = seq * n_batch
    tiles = _row_segments(tm, n_i, seq, n_batch, m - n_p)
    kern = functools.partial(_ffn_down_kernel, nk=nk, tiles=tiles)
    return pl.pallas_call(
        kern, grid=(n_i, n_j, nk),
        in_specs=[pl.BlockSpec((tm, tk), lambda i, j, k: (i, k)),
                  pl.BlockSpec((tk, tn), lambda i, j, k: (k, j)),
                  pl.BlockSpec((tm, tn), lambda i, j, k: (i, j)),
                  pl.BlockSpec((n_batch, 1, tn), lambda i, j, k: (0, 0, j)),
                  pl.BlockSpec((DEC_ROWS, tn), lambda i, j, k: (0, j))],
        out_specs=[pl.BlockSpec((tm, tn), lambda i, j, k: (i, j)),
                   pl.BlockSpec((DEC_ROWS, tn), lambda i, j, k: (0, jnp.where(i == n_i - 1, j, 0)))],
        out_shape=[jax.ShapeDtypeStruct((n_p, D_MODEL), F32), jax.ShapeDtypeStruct((m - n_p, D_MODEL), F32)],
        compiler_params=_params(3), name="ffn_down")(u, w, x, gate_p, gate_s)


def _place_kernel(big_ref, small_ref, o_ref):
    o_ref[...] = small_ref[...]


def _place_rows(big, small, row_block, name):
    return pl.pallas_call(
        _place_kernel, grid=(1,),
        in_specs=[pl.BlockSpec(memory_space=pl.ANY), pl.BlockSpec(small.shape, lambda i: (0, 0))],
        out_specs=pl.BlockSpec(small.shape, lambda i: (row_block, 0)),
        out_shape=jax.ShapeDtypeStruct(big.shape, big.dtype),
        input_output_aliases={0: 0}, compiler_params=_params(1), name=name)(big, small)


NORM_ROWS = 512


def _norm_mod_kernel(xp_ref, xs_ref, g_ref, scp_ref, shp_ref, scs_ref, shs_ref, o_ref, *, n_i):
    i = pl.program_id(0)

    def apply(x, sc, sh):
        y = x * lax.rsqrt(jnp.mean(x * x, axis=-1, keepdims=True) + EPS) * g_ref[...]
        return (y * (1.0 + sc) + sh).astype(o_ref.dtype)

    @pl.when(i < n_i - 1)
    def _():
        o_ref[...] = apply(xp_ref[...], scp_ref[0], shp_ref[0])

    @pl.when(i == n_i - 1)
    def _():
        o_ref[:DEC_ROWS, :] = apply(xs_ref[...], scs_ref[...], shs_ref[...])


def _norm_mod(x_p, x_s, s_block, g, sc_p, sh_p, sc_s, sh_s, *, n_p, rows_per_batch, name):
    d = x_p.shape[1]
    tm = NORM_ROWS
    n_i = n_p // tm + 1
    tpb = rows_per_batch // tm
    nb = sc_p.shape[0]
    pspec = pl.BlockSpec((1, 1, d), lambda i: (jnp.minimum(i // tpb, nb - 1), 0, 0))
    sspec = pl.BlockSpec((DEC_ROWS, d), lambda i: (0, 0))
    return pl.pallas_call(
        functools.partial(_norm_mod_kernel, n_i=n_i), grid=(n_i,),
        in_specs=[pl.BlockSpec((tm, d), lambda i: (jnp.minimum(i, n_i - 2), 0)),
                  pl.BlockSpec((DEC_ROWS, d), lambda i: (s_block, 0)),
                  pl.BlockSpec((1, d), lambda i: (0, 0)), pspec, pspec, sspec, sspec],
        out_specs=pl.BlockSpec((tm, d), lambda i: (i, 0)),
        out_shape=jax.ShapeDtypeStruct((n_p + DEC_ROWS, d), BF16),
        compiler_params=_params(1), name=name)(x_p, x_s, g, sc_p, sh_p, sc_s, sh_s)


def _norm_rope_kernel(x_ref, c_ref, s_ref, g_ref, o_ref, *, n_heads):
    c, s, g = c_ref[...], s_ref[...], g_ref[...]
    for h in range(n_heads):
        sl = slice(h * HEAD_DIM, (h + 1) * HEAD_DIM)
        o_ref[:, sl] = _norm_rope(x_ref[:, sl], g, c, s)


def _norm_rope_rows(x, row_block, c, s, g, *, n_heads, name):
    w = n_heads * HEAD_DIM
    return pl.pallas_call(
        functools.partial(_norm_rope_kernel, n_heads=n_heads), grid=(1,),
        in_specs=[pl.BlockSpec((DEC_ROWS, w), lambda i: (row_block, 0)),
                  pl.BlockSpec((DEC_ROWS, HEAD_DIM), lambda i: (0, 0)),
                  pl.BlockSpec((DEC_ROWS, HEAD_DIM), lambda i: (0, 0)),
                  pl.BlockSpec((1, HEAD_DIM), lambda i: (0, 0))],
        out_specs=pl.BlockSpec((DEC_ROWS, w), lambda i: (0, 0)),
        out_shape=jax.ShapeDtypeStruct((DEC_ROWS, w), F32),
        compiler_params=_params(1), name=name)(x, c, s, g)


def _k_prep_kernel(k_ref, v_ref, c_ref, s_ref, g_ref, ko_ref, ka_ref, va_ref, km_ref, *, n_blk):
    blk = pl.program_id(0) % n_blk
    c, s, g = c_ref[...], s_ref[...], g_ref[...]
    lane = lax.broadcasted_iota(jnp.int32, (MOBA_BLOCK, HEAD_DIM), 1)
    block_tag = jnp.where(lane == blk, 1.0, 0.0).astype(BF16)
    ones_col = jnp.where(lane == 0, 1.0, 0.0).astype(BF16)
    for h in range(KV_HEADS):
        sl = slice(h * HEAD_DIM, (h + 1) * HEAD_DIM)
        lo, mid, hi = 2 * h * HEAD_DIM, (2 * h + 1) * HEAD_DIM, (2 * h + 2) * HEAD_DIM
        kr = _norm_rope(k_ref[:, sl], g, c, s)
        ko_ref[:, sl] = kr
        ka_ref[:, lo:mid] = kr.astype(BF16)
        ka_ref[:, mid:hi] = block_tag
        va_ref[:, lo:mid] = v_ref[:, sl].astype(BF16)
        va_ref[:, mid:hi] = ones_col
        km_ref[0, :, sl] = jnp.mean(kr, axis=0, keepdims=True)


def _k_prep(kv, c, s, g, *, n_rows, n_blk):
    nt = n_rows // MOBA_BLOCK
    return pl.pallas_call(
        functools.partial(_k_prep_kernel, n_blk=n_blk), grid=(nt,),
        in_specs=[pl.BlockSpec((MOBA_BLOCK, KV_DIM), lambda t: (t, 0)),
                  pl.BlockSpec((MOBA_BLOCK, KV_DIM), lambda t: (t, 1)),
                  pl.BlockSpec((MOBA_BLOCK, HEAD_DIM), lambda t: (t % n_blk, 0)),
                  pl.BlockSpec((MOBA_BLOCK, HEAD_DIM), lambda t: (t % n_blk, 0)),
                  pl.BlockSpec((1, HEAD_DIM), lambda t: (0, 0))],
        out_specs=[pl.BlockSpec((MOBA_BLOCK, KV_DIM), lambda t: (t, 0)),
                   pl.BlockSpec((MOBA_BLOCK, 2 * KV_DIM), lambda t: (t, 0)),
                   pl.BlockSpec((MOBA_BLOCK, 2 * KV_DIM), lambda t: (t, 0)),
                   pl.BlockSpec((1, 1, KV_DIM), lambda t: (t, 0, 0))],
        out_shape=[jax.ShapeDtypeStruct((n_rows, KV_DIM), F32),
                   jax.ShapeDtypeStruct((n_rows, 2 * KV_DIM), BF16),
                   jax.ShapeDtypeStruct((n_rows, 2 * KV_DIM), BF16),
                   jax.ShapeDtypeStruct((nt, 1, KV_DIM), F32)],
        compiler_params=_params(1), name="k_prep")(kv, kv, c, s, g)


def _q_prep_kernel(q_ref, c_ref, s_ref, g_ref, km_ref, o_ref, *, n_blk):
    own = pl.program_id(0) % n_blk
    c, s, g = c_ref[...], s_ref[...], g_ref[...]
    n_idx = lax.broadcasted_iota(jnp.int32, (n_blk, MOBA_BLOCK), 0)
    lane = lax.broadcasted_iota(jnp.int32, (MOBA_BLOCK, HEAD_DIM), 1)
    past = n_idx < own
    km = km_ref[0]
    pad = jnp.zeros((HEAD_DIM - n_blk, MOBA_BLOCK), F32)
    for h in range(N_HEADS):
        kv = h // KV_REP
        qh = _norm_rope(q_ref[:, h * HEAD_DIM:(h + 1) * HEAD_DIM], g, c, s)
        gate = _dot_hp(km[:, kv * HEAD_DIM:(kv + 1) * HEAD_DIM], qh, (((1,), (1,)), ((), ())))
        gate = jnp.where(past, gate, -jnp.inf)
        sel = (_topk_select(gate, n_idx, n_blk) & past) | (n_idx == own)
        sel_t = jnp.concatenate([jnp.where(sel, 1.0, 0.0), pad], axis=0).T
        bias = jnp.where(lane < n_blk, (sel_t - 1.0) * (-MASK_SCORE), 0.0)
        o_ref[:, 2 * h * HEAD_DIM:(2 * h + 1) * HEAD_DIM] = (qh * ATTN_SCALE).astype(BF16)
        o_ref[:, (2 * h + 1) * HEAD_DIM:(2 * h + 2) * HEAD_DIM] = bias.astype(BF16)


def _q_prep(q, c, s, g, kmean, *, n_rows, n_blk):
    nt = n_rows // MOBA_BLOCK
    return pl.pallas_call(
        functools.partial(_q_prep_kernel, n_blk=n_blk), grid=(nt,),
        in_specs=[pl.BlockSpec((MOBA_BLOCK, Q_DIM), lambda t: (t, 0)),
                  pl.BlockSpec((MOBA_BLOCK, HEAD_DIM), lambda t: (t % n_blk, 0)),
                  pl.BlockSpec((MOBA_BLOCK, HEAD_DIM), lambda t: (t % n_blk, 0)),
                  pl.BlockSpec((1, HEAD_DIM), lambda t: (0, 0)),
                  pl.BlockSpec((1, n_blk, KV_DIM), lambda t: (t // n_blk, 0, 0))],
        out_specs=pl.BlockSpec((MOBA_BLOCK, 2 * Q_DIM), lambda t: (t, 0)),
        out_shape=jax.ShapeDtypeStruct((n_rows, 2 * Q_DIM), BF16),
        compiler_params=_params(1), name="q_prep")(q, c, s, g, kmean)


KEY_GROUP = 4 * MOBA_BLOCK


def _attn_p_kernel(q_ref, k_ref, v_ref, o_ref, m_ref, acc_ref):
    i = pl.program_id(2)
    rows = KV_REP * MOBA_BLOCK
    aug = 2 * HEAD_DIM
    qs = jnp.concatenate([q_ref[:, r * aug:(r + 1) * aug] for r in range(KV_REP)], axis=0)
    dims = (((1,), (1,)), ((), ()))
    blocks_per_group = KEY_GROUP // MOBA_BLOCK
    n_full = i // blocks_per_group
    group_start = pl.multiple_of(n_full * KEY_GROUP, KEY_GROUP)

    def own_group(n_blocks):
        width = n_blocks * MOBA_BLOCK
        kg, vg = k_ref[pl.ds(group_start, width), :], v_ref[pl.ds(group_start, width), :]
        n_chains = 1 if n_blocks == 1 else 2
        part = rows // n_chains
        qpos = (n_blocks - 1) * MOBA_BLOCK + (lax.broadcasted_iota(jnp.int32, (part, width), 0) & (MOBA_BLOCK - 1))
        visible = lax.broadcasted_iota(jnp.int32, (part, width), 1) <= qpos
        for c in range(n_chains):
            sl = slice(c * part, (c + 1) * part)
            s = jnp.where(visible, lax.dot_general(qs[sl], kg, dims, preferred_element_type=F32), -jnp.inf)
            m = jnp.max(s, axis=1, keepdims=True)
            m_ref[sl, :] = jnp.broadcast_to(m, (part, LANES))
            acc_ref[sl, :] = jnp.dot(jnp.exp(s - m).astype(BF16), vg, preferred_element_type=F32)

    half = rows // 2
    for n_blocks in range(1, blocks_per_group + 1):
        pl.when(i % blocks_per_group == n_blocks - 1)(functools.partial(own_group, n_blocks))

    def body(gi, carry):
        st = pl.multiple_of(gi * KEY_GROUP, KEY_GROUP)
        kg, vg = k_ref[pl.ds(st, KEY_GROUP), :], v_ref[pl.ds(st, KEY_GROUP), :]
        out = []
        for hh in range(2):
            m, acc = carry[2 * hh], carry[2 * hh + 1]
            s = lax.dot_general(qs[hh * half:(hh + 1) * half], kg, dims, preferred_element_type=F32)
            m_new = jnp.maximum(m, jnp.max(s, axis=1, keepdims=True))
            p = jnp.exp(s - jnp.tile(m_new, (1, KEY_GROUP // LANES)))
            pv = jnp.dot(p.astype(BF16), vg, preferred_element_type=F32)
            out += [m_new, acc * jnp.tile(jnp.exp(m - m_new), (1, aug // LANES)) + pv]
        return tuple(out)

    init = (m_ref[:half, :], acc_ref[:half, :], m_ref[half:, :], acc_ref[half:, :])
    fin = lax.fori_loop(0, n_full, body, init)
    acc = jnp.concatenate([fin[1], fin[3]], axis=0)
    o = acc[:, :HEAD_DIM] / acc[:, HEAD_DIM:HEAD_DIM + 1]
    for r in range(KV_REP):
        o_ref[:, r * HEAD_DIM:(r + 1) * HEAD_DIM] = o[r * MOBA_BLOCK:(r + 1) * MOBA_BLOCK].astype(o_ref.dtype)


def _attn_prompt(q_aug, k_aug, v_aug, *, n_batch, seq, out_rows):
    n_blk = seq // MOBA_BLOCK
    aug = 2 * HEAD_DIM
    return pl.pallas_call(
        _attn_p_kernel, grid=(n_batch, KV_HEADS, n_blk),
        in_specs=[pl.BlockSpec((MOBA_BLOCK, KV_REP * aug), lambda b, g, i: (b * n_blk + i, g)),
                  pl.BlockSpec((seq, aug), lambda b, g, i: (b, g)),
                  pl.BlockSpec((seq, aug), lambda b, g, i: (b, g))],
        out_specs=pl.BlockSpec((MOBA_BLOCK, KV_REP * HEAD_DIM), lambda b, g, i: (b * n_blk + i, g)),
        out_shape=jax.ShapeDtypeStruct((out_rows, Q_DIM), BF16),
        scratch_shapes=[pltpu.VMEM((KV_REP * MOBA_BLOCK, LANES), F32), pltpu.VMEM((KV_REP * MOBA_BLOCK, aug), F32)],
        compiler_params=_params(3), name="attn_prompt")(q_aug, k_aug, v_aug)


GROUP_ROWS = 16
PAGES_PER_STEP = 8


def _diag_blocks(full):
    return jnp.concatenate([full[g * GROUP_ROWS:(g + 1) * GROUP_ROWS, g * HEAD_DIM:(g + 1) * HEAD_DIM]
                            for g in range(KV_HEADS)], axis=0)


def _page_tile(ref):
    return jnp.concatenate([ref[0, pl.ds(g, PAGE_SIZE, stride=KV_HEADS), :] for g in range(KV_HEADS)], axis=1)


def _attn_s_kernel(pt_ref, *refs, n_blk, dec_seq):
    n_pg = PAGES_PER_STEP
    k_refs, v_refs = refs[:n_pg], refs[n_pg:2 * n_pg]
    qb_ref, qf_ref, ko_ref, vo_ref, o_ref, ob_ref, m_ref, l_ref, ks_ref = refs[2 * n_pg:]
    step = pl.program_id(1)
    pages_per_block = MOBA_BLOCK // PAGE_SIZE
    blocks_per_step = n_pg // pages_per_block
    qb = qb_ref[0]
    for u in range(blocks_per_step):
        jb = step * blocks_per_step + u
        pages = range(u * pages_per_block, (u + 1) * pages_per_block)
        kblk = jnp.concatenate([_page_tile(k_refs[p]) for p in pages], axis=0)
        vblk = jnp.concatenate([_page_tile(v_refs[p]) for p in pages], axis=0).astype(BF16)
        ks_ref[pl.ds(jb, 1), :] = jnp.sum(kblk, axis=0, keepdims=True)
        st = jnp.dot(kblk.astype(BF16), qb, preferred_element_type=F32)
        mb = jnp.max(st, axis=0, keepdims=True)
        p = jnp.exp(st - mb)
        m_ref[pl.ds(jb, 1), :] = mb
        l_ref[pl.ds(jb, 1), :] = jnp.sum(p, axis=0, keepdims=True)
        ob_ref[jb] = _diag_blocks(jnp.dot(p.T.astype(BF16), vblk, preferred_element_type=F32))

    @pl.when(step == n_blk // blocks_per_step - 1)
    def _():
        gate = _dot_hp(ks_ref[...] * (1.0 / MOBA_BLOCK), qf_ref[0], (((1,), (0,)), ((), ())))
        n_idx = lax.broadcasted_iota(jnp.int32, (n_blk, LANES), 0)
        sel = _topk_select(gate, n_idx, n_blk)
        m_all = m_ref[...]
        so = jnp.dot(ko_ref[0].astype(BF16), qb, preferred_element_type=F32)
        krow = lax.broadcasted_iota(jnp.int32, so.shape, 0)
        tq = (lax.broadcasted_iota(jnp.int32, so.shape, 1) >> 2) & (dec_seq - 1)
        so = jnp.where((krow <= tq) & (krow < dec_seq), so, -jnp.inf)
        m_tot = jnp.maximum(jnp.max(jnp.where(sel, m_all, -jnp.inf), axis=0, keepdims=True),
                            jnp.max(so, axis=0, keepdims=True))
        w = jnp.where(sel, jnp.exp(m_all - m_tot), 0.0)
        po = jnp.exp(so - m_tot)
        inv = 1.0 / (jnp.sum(w * l_ref[...], axis=0, keepdims=True) + jnp.sum(po, axis=0, keepdims=True))
        pad = jnp.zeros((LANES - n_blk - so.shape[0], LANES), F32)
        wt = jnp.concatenate([w * inv, po * inv, pad], axis=0).T
        acc = jnp.zeros((LANES, HEAD_DIM), F32)
        for b in range(n_blk):
            acc = acc + wt[:, b:b + 1] * ob_ref[b]
        vo = vo_ref[0]
        for t in range(dec_seq):
            vrow = jnp.concatenate(
                [jnp.broadcast_to(vo[t:t + 1, g * HEAD_DIM:(g + 1) * HEAD_DIM], (GROUP_ROWS, HEAD_DIM))
                 for g in range(KV_HEADS)], axis=0)
            acc = acc + wt[:, n_blk + t:n_blk + t + 1] * vrow
        o_ref[0] = acc


def _attn_sample(page_table, cache_k, cache_v, qbd, qbd_f32, k_own, v_own, *, dec_seq):
    n_seq, n_pages = page_table.shape
    n_blk = n_pages * PAGE_SIZE // MOBA_BLOCK
    page_rows = PAGE_SIZE * KV_HEADS
    n_pg = PAGES_PER_STEP
    assert n_pages % n_pg == 0
    pages = [pl.BlockSpec((1, page_rows, HEAD_DIM), lambda s, t, pt, p=p: (pt[s, t * n_pg + p], 0, 0))
             for p in range(n_pg)]
    per_seq = lambda r, c: pl.BlockSpec((1, r, c), lambda s, t, pt: (s, 0, 0))
    grid_spec = pltpu.PrefetchScalarGridSpec(
        num_scalar_prefetch=1, grid=(n_seq, n_pages // n_pg),
        in_specs=pages + pages + [per_seq(KV_DIM, LANES), per_seq(KV_DIM, LANES),
                                  per_seq(8, KV_DIM), per_seq(8, KV_DIM)],
        out_specs=per_seq(LANES, HEAD_DIM),
        scratch_shapes=[pltpu.VMEM((n_blk, LANES, HEAD_DIM), F32), pltpu.VMEM((n_blk, LANES), F32),
                        pltpu.VMEM((n_blk, LANES), F32), pltpu.VMEM((n_blk, KV_DIM), F32)])
    return pl.pallas_call(
        functools.partial(_attn_s_kernel, n_blk=n_blk, dec_seq=dec_seq), grid_spec=grid_spec,
        out_shape=jax.ShapeDtypeStruct((n_seq, LANES, HEAD_DIM), F32),
        compiler_params=_params(2), name="attn_sample")(
            page_table, *([cache_k] * n_pg), *([cache_v] * n_pg), qbd, qbd_f32, k_own, v_own)


CONV_PAD = 8
SHORT_CHUNK = 16


def _chunk_t(x, L):
    if L == LANES:
        return x.T
    return jnp.concatenate([x, jnp.zeros((LANES - L, LANES), F32)], axis=0).T[:, :L]


def _ssd_kernel(xbc_ref, z_ref, dt_ref, cprev_ref, h0_ref, wc_ref, bc_ref, dtb_ref, alog_ref, dx_ref, gs_ref,
                e2_ref, e1_ref, tril_ref, y_ref, hf_ref, cnew_ref, ht_ref, cbuf_ref, *, L, nc, n_valid, rows_in):
    c = pl.program_id(1)
    gw = SSM_REP * SSM_HEAD_DIM

    @pl.when(c == 0)
    def _():
        cbuf_ref[0:CONV_PAD, :] = cprev_ref[0]
        if rows_in < L:
            cbuf_ref[CONV_PAD + rows_in:, :] = jnp.zeros((L - rows_in, CONV_DIM), F32)
        for g in range(SSM_GROUPS):
            ht_ref[g] = h0_ref[0, g * SSM_REP:(g + 1) * SSM_REP].reshape(gw, SSM_STATE).T

    cbuf_ref[CONV_PAD:CONV_PAD + rows_in, :] = xbc_ref[...]
    first = CONV_PAD - (CONV_W - 1)
    conv = bc_ref[...]
    for j in range(CONV_W):
        conv = conv + cbuf_ref[first + j:first + j + L, :] * wc_ref[j:j + 1, :]
    cnew_ref[0] = cbuf_ref[n_valid:CONV_PAD + n_valid, :]
    cbuf_ref[first:CONV_PAD, :] = cbuf_ref[first + L:CONV_PAD + L, :]

    act = conv * jax.nn.sigmoid(conv)
    xs = act[:, :SSM_INNER]
    if rows_in < L:
        zpad = jnp.zeros((L - rows_in, SSM_INNER), F32)
        z = jnp.concatenate([z_ref[...], zpad], axis=0)
        dt_raw = jnp.concatenate([dt_ref[...], jnp.zeros((L - rows_in, LANES), F32)], axis=0)
    else:
        z, dt_raw = z_ref[...], dt_ref[...]

    row = lax.broadcasted_iota(jnp.int32, (L, LANES), 0)
    lane = lax.broadcasted_iota(jnp.int32, (L, LANES), 1)
    dt = jax.nn.softplus(dt_raw + dtb_ref[...])
    dt = jnp.where((row < n_valid) & (lane < SSM_HEADS), dt, 0.0)
    cum = _dot3_lhs_exact(tril_ref[...], dt * (-jnp.exp(alog_ref[...])))
    cum_t, dt_t = _chunk_t(cum, L), _chunk_t(dt, L)
    causal = lax.broadcasted_iota(jnp.int32, (L, L), 1) <= lax.broadcasted_iota(jnp.int32, (L, L), 0)

    outs = []
    for g in range(SSM_GROUPS):
        sl = slice(g * gw, (g + 1) * gw)
        bg = act[:, SSM_INNER + g * SSM_STATE:SSM_INNER + (g + 1) * SSM_STATE]
        cg = act[:, SSM_INNER + (SSM_GROUPS + g) * SSM_STATE:SSM_INNER + (SSM_GROUPS + g + 1) * SSM_STATE]
        bg_b, cg_b = bg.astype(BF16), cg.astype(BF16)
        cb = lax.dot_general(cg_b, bg_b, (((1,), (1,)), ((), ())), preferred_element_type=F32)
        cum_x2 = _dot3_rhs_exact(cum, e2_ref[:, g * SSM_REP * LANES:(g + 1) * SSM_REP * LANES])
        cum_x = _dot3_rhs_exact(cum, e1_ref[:, sl])
        dt_x = _dot3_rhs_exact(dt, e1_ref[:, sl])
        ht = ht_ref[g]
        y_inter = jnp.dot(cg_b, ht.astype(BF16), preferred_element_type=F32)
        parts = []
        for r in range(SSM_REP):
            h = g * SSM_REP + r
            seg = cum_x2[:, r * LANES:r * LANES + L] - cum_t[h:h + 1, :]
            w = cb * jnp.exp(jnp.where(causal, seg, -jnp.inf)) * dt_t[h:h + 1, :]
            xh = xs[:, h * SSM_HEAD_DIM:(h + 1) * SSM_HEAD_DIM].astype(BF16)
            parts.append(jnp.dot(w.astype(BF16), xh, preferred_element_type=F32))
        xg = xs[:, sl]
        yg = jnp.concatenate(parts, axis=1) + y_inter * jnp.exp(cum_x) + dx_ref[:, sl] * xg
        last = cum_x[L - 1:L, :]
        xw = (xg * (jnp.exp(last - cum_x) * dt_x)).astype(BF16)
        ht_ref[g] = ht * jnp.exp(last) + jnp.dot(_chunk_t(bg, L).astype(BF16), xw, preferred_element_type=F32)
        zg = z[:, sl]
        yg = yg * (zg * jax.nn.sigmoid(zg))
        yg = yg * lax.rsqrt(jnp.mean(yg * yg, axis=-1, keepdims=True) + EPS) * gs_ref[:, sl]
        outs.append(yg[:rows_in].astype(y_ref.dtype))
    y_ref[...] = jnp.concatenate(outs, axis=1)

    @pl.when(c == nc - 1)
    def _():
        for g in range(SSM_GROUPS):
            hf_ref[0, g * SSM_REP:(g + 1) * SSM_REP] = ht_ref[g].T.reshape(SSM_REP, SSM_HEAD_DIM, SSM_STATE)


def _ssd(xbc, z, dt, conv_prev, h0, w_conv, b_conv, dt_bias, a_log, d_skip, g_ssm, *, n_seq, nc, rows_in, n_valid,
         out_rows, chunk):
    L = chunk
    head_of = jnp.arange(SSM_HEADS * LANES, dtype=jnp.int32) // LANES
    e2 = (jnp.arange(LANES, dtype=jnp.int32)[:, None] == head_of[None, :]).astype(BF16)
    chan_head = jnp.arange(SSM_INNER, dtype=jnp.int32) // SSM_HEAD_DIM
    e1 = (jnp.arange(LANES, dtype=jnp.int32)[:, None] == chan_head[None, :]).astype(BF16)
    tril = (jnp.arange(L)[:, None] >= jnp.arange(L)[None, :]).astype(BF16)
    pad_l = lambda v: jnp.pad(v.reshape(1, -1), ((0, 0), (0, LANES - v.shape[0])))
    dx = jnp.repeat(d_skip, SSM_HEAD_DIM).reshape(1, SSM_INNER)
    const = lambda shape: pl.BlockSpec(shape, lambda b, c: (0,) * len(shape))
    kern = functools.partial(_ssd_kernel, L=L, nc=nc, n_valid=n_valid, rows_in=rows_in)
    return pl.pallas_call(
        kern, grid=(n_seq, nc),
        in_specs=[pl.BlockSpec((rows_in, CONV_DIM), lambda b, c: (b * nc + c, 0)),
                  pl.BlockSpec((rows_in, SSM_INNER), lambda b, c: (b * nc + c, 0)),
                  pl.BlockSpec((rows_in, LANES), lambda b, c: (b * nc + c, 0)),
                  pl.BlockSpec((1, CONV_PAD, CONV_DIM), lambda b, c: (b, 0, 0)),
                  pl.BlockSpec((1, SSM_HEADS, SSM_HEAD_DIM, SSM_STATE), lambda b, c: (b, 0, 0, 0)),
                  const((CONV_W, CONV_DIM)), const((1, CONV_DIM)), const((1, LANES)), const((1, LANES)),
                  const((1, SSM_INNER)), const((1, SSM_INNER)),
                  const((LANES, SSM_HEADS * LANES)), const((LANES, SSM_INNER)), const((L, L))],
        out_specs=[pl.BlockSpec((rows_in, SSM_INNER), lambda b, c: (b * nc + c, 0)),
                   pl.BlockSpec((1, SSM_HEADS, SSM_HEAD_DIM, SSM_STATE), lambda b, c: (b, 0, 0, 0)),
                   pl.BlockSpec((1, CONV_PAD, CONV_DIM), lambda b, c: (b, 0, 0))],
        out_shape=[jax.ShapeDtypeStruct((out_rows, SSM_INNER), BF16),
                   jax.ShapeDtypeStruct((n_seq, SSM_HEADS, SSM_HEAD_DIM, SSM_STATE), F32),
                   jax.ShapeDtypeStruct((n_seq, CONV_PAD, CONV_DIM), F32)],
        scratch_shapes=[pltpu.VMEM((SSM_GROUPS, SSM_STATE, SSM_REP * SSM_HEAD_DIM), F32),
                        pltpu.VMEM((CONV_PAD + L, CONV_DIM), F32)],
        compiler_params=_params(2), name="ssd")(
            xbc, z, dt, conv_prev, h0, w_conv, b_conv.reshape(1, -1), pad_l(dt_bias), pad_l(a_log), dx,
            g_ssm.reshape(1, -1), e2, e1, tril)


def kernel(x_prompt, x_sample, cache_k, cache_v, state_ssm, state_conv, page_table, c_prompt, c_sample, w_ada, b_ada, g_norm1, w_in, q_norm_g, k_norm_g, w_conv, b_conv, dt_bias, a_log, d_skip, g_ssm, w_attn_o, w_ssm_o, w_out, g_norm2, w_up, w_down):
    n_batch, seq, d = x_prompt.shape
    n_dec, dec_seq, _ = x_sample.shape
    n_p, n_s = n_batch * seq, n_dec * dec_seq
    n_rows = n_p + n_s
    assert n_s == DEC_ROWS and d == D_MODEL and seq % KEY_GROUP == 0 and n_p % DEC_ROWS == 0
    n_pool = cache_k.shape[0]
    past_len = page_table.shape[1] * PAGE_SIZE
    n_blk = seq // MOBA_BLOCK
    sample_block = n_p // DEC_ROWS
    xp2, xs2 = x_prompt.reshape(n_p, D_MODEL), x_sample.reshape(n_s, D_MODEL)
    row1 = lambda v: v.reshape(1, -1)

    c_all = jnp.concatenate([c_prompt, c_sample], axis=0)
    c_pad = jnp.pad(c_all, ((0, -c_all.shape[0] % 16), (0, 0))).astype(BF16)
    tn = PROJ_TN
    mod = _mm(c_pad, w_ada, n_cols=6 * D_MODEL, n_i=1, tn=tn, out_dtype=F32, epi=_epi_bias,
              extras=(row1(b_ada),), extra_specs=[pl.BlockSpec((1, tn), lambda i, j: (0, j))], name="ada_mod")
    mod = mod[:n_batch + n_dec].reshape(n_batch + n_dec, 6, D_MODEL)
    sh1, sc1, gt1, sh2, sc2, gt2 = (
        (mod[:n_batch, i].reshape(n_batch, 1, D_MODEL), jnp.repeat(mod[n_batch:, i], dec_seq, axis=0))
        for i in range(6))

    h = _norm_mod(xp2, xs2, 0, row1(g_norm1), sc1[0], sh1[0], sc1[1], sh1[1], n_p=n_p, rows_per_batch=seq,
                  name="norm_mod1")
    w_in_t = w_in.T
    proj = functools.partial(_mm, h, w_in_t, w_is_t=True, n_i=ROW_TILES, out_dtype=F32)
    q = proj(n_cols=Q_DIM, col_off=0, tn=tn, epi=_epi_store, name="in_q")
    kv = proj(n_cols=2 * KV_DIM, col_off=Q_DIM, tn=tn, epi=_epi_store, name="in_kv")
    z = proj(n_cols=SSM_INNER, col_off=Q_DIM + 2 * KV_DIM, tn=tn, epi=_epi_store, name="in_z")
    xbc = proj(n_cols=CONV_DIM, col_off=Q_DIM + 2 * KV_DIM + SSM_INNER, tn=tn, epi=_epi_store, name="in_xbc")
    dt_off = Q_DIM + 2 * KV_DIM + SSM_INNER + CONV_DIM
    dt = proj(n_cols=LANES, col_off=dt_off, tn=LANES, epi=_epi_store, name="in_dt")
    gates = proj(n_cols=2 * D_MODEL, col_off=dt_off, shift=SSM_HEADS, tn=tn, epi=_epi_sigmoid, name="in_gates")

    cp, sp = _rope_tables(jnp.arange(seq, dtype=jnp.int32))
    k_p, k_aug, v_aug, kmean = _k_prep(kv, cp, sp, row1(k_norm_g), n_rows=n_p, n_blk=n_blk)
    q_aug = _q_prep(q, cp, sp, row1(q_norm_g), kmean.reshape(n_batch, n_blk, KV_DIM), n_rows=n_p, n_blk=n_blk)
    attn = _attn_prompt(q_aug, k_aug, v_aug, n_batch=n_batch, seq=seq, out_rows=n_rows)

    pos_s = past_len + (jnp.arange(n_s, dtype=jnp.int32) % dec_seq)
    cs, ss = _rope_tables(pos_s)
    q_s = _norm_rope_rows(q, sample_block, cs, ss, row1(q_norm_g), n_heads=N_HEADS, name="q_sample")
    k_s = _norm_rope_rows(kv, sample_block, cs, ss, row1(k_norm_g), n_heads=KV_HEADS, name="k_sample")
    v_s = kv[n_p:, KV_DIM:]
    q_t = q_s.reshape(n_dec, dec_seq, KV_HEADS, KV_REP, HEAD_DIM).transpose(0, 2, 4, 1, 3)
    q_t = q_t.reshape(n_dec, KV_HEADS, HEAD_DIM, GROUP_ROWS)
    eye = jnp.eye(KV_HEADS, dtype=F32)
    qbd_f32 = (q_t[:, :, :, None, :] * eye[None, :, None, :, None]).reshape(n_dec, KV_DIM, LANES)
    qbd = (qbd_f32 * ATTN_SCALE).astype(BF16)
    own_pad = lambda v: jnp.pad(v.reshape(n_dec, dec_seq, KV_DIM), ((0, 0), (0, 8 - dec_seq), (0, 0)))
    pool_view = lambda cache: cache.reshape(n_pool, PAGE_SIZE * KV_HEADS, HEAD_DIM)
    o_s = _attn_sample(page_table, pool_view(cache_k), pool_view(cache_v), qbd, qbd_f32, own_pad(k_s), own_pad(v_s),
                       dec_seq=dec_seq)
    attn_s = o_s.reshape(n_dec, KV_HEADS, dec_seq, KV_REP, HEAD_DIM).transpose(0, 2, 1, 3, 4)
    attn = _place_rows(attn, attn_s.reshape(n_s, Q_DIM).astype(BF16), sample_block, "attn_place")

    ssd = functools.partial(_ssd, w_conv=w_conv, b_conv=b_conv, dt_bias=dt_bias, a_log=a_log, d_skip=d_skip,
                            g_ssm=g_ssm)
    conv_pad = lambda v: jnp.pad(v, ((0, 0), (CONV_PAD - (CONV_W - 1), 0), (0, 0)))
    zeros_h = jnp.zeros((n_batch, SSM_HEADS, SSM_HEAD_DIM, SSM_STATE), F32)
    zeros_c = jnp.zeros((n_batch, CONV_W - 1, CONV_DIM), F32)
    y_ssm, ssm_p, cnew_p = ssd(xbc, z, dt, conv_pad(zeros_c), zeros_h, n_seq=n_batch, nc=seq // SSM_CHUNK,
                               rows_in=SSM_CHUNK, n_valid=SSM_CHUNK, out_rows=n_rows, chunk=SSM_CHUNK)
    short = SHORT_CHUNK
    pad_s = lambda v: jnp.pad(v[n_p:].reshape(n_dec, dec_seq, -1), ((0, 0), (0, short - dec_seq), (0, 0))).reshape(
        n_dec * short, -1)
    y_s, ssm_s, cnew_s = ssd(pad_s(xbc), pad_s(z), pad_s(dt), conv_pad(state_conv), state_ssm, n_seq=n_dec, nc=1,
                             rows_in=short, n_valid=dec_seq, out_rows=n_dec * short, chunk=short)
    y_s = y_s.reshape(n_dec, short, SSM_INNER)[:, :dec_seq].reshape(n_s, SSM_INNER)
    y_ssm = _place_rows(y_ssm, y_s, sample_block, "ssm_place")

    n_i = GATED_ROW_TILES
    tm = n_rows // n_i
    tile = lambda off: pl.BlockSpec((tm, tn), lambda i, j: (i, j + off))
    mix_a = _mm(attn, w_attn_o, n_cols=D_MODEL, n_i=n_i, tn=tn, out_dtype=F32, epi=_epi_gate,
                extras=(gates,), extra_specs=[tile(0)], name="attn_proj")
    mixed = _mm(y_ssm, w_ssm_o, n_cols=D_MODEL, n_i=n_i, tn=tn, out_dtype=BF16, epi=_epi_gate_add,
                extras=(mix_a, gates), extra_specs=[tile(0), tile(D_MODEL // tn)], name="ssm_proj")
    assert n_rows - n_s == n_p and tm >= n_s
    res_specs = [tile(0), pl.BlockSpec((DEC_ROWS, tn), lambda i, j: (0, j)),
                 pl.BlockSpec((n_batch, 1, tn), lambda i, j: (0, 0, j)),
                 pl.BlockSpec((DEC_ROWS, tn), lambda i, j: (0, j))]
    x1 = _mm(mixed, w_out, n_cols=D_MODEL, n_i=n_i, tn=tn, out_dtype=F32, epi=_make_epi_residual(tm, seq, n_batch),
             extras=(xp2, xs2, gt1[0], gt1[1]), extra_specs=res_specs, name="out_proj")

    h2 = _norm_mod(x1, x1, sample_block, row1(g_norm2), sc2[0], sh2[0], sc2[1], sh2[1], n_p=n_p,
                   rows_per_batch=seq, name="norm_mod2")
    u = _mm(h2, w_up, n_cols=D_FF, n_i=ROW_TILES, tn=PROJ_TN, out_dtype=BF16, epi=_epi_relu2, name="ffn_up")
    y_p, y_s2 = _ffn_down(u, w_down, x1, gt2[0], gt2[1], seq=seq, n_batch=n_batch, tn=512, tk=2048)

    kv4 = lambda v, b, t: v.reshape(b, t, KV_HEADS, HEAD_DIM)
    first = CONV_PAD - (CONV_W - 1)
    return (y_p.reshape(n_batch, seq, D_MODEL), y_s2.reshape(n_dec, dec_seq, D_MODEL),
            kv4(k_p, n_batch, seq), kv4(kv[:n_p, KV_DIM:], n_batch, seq),
            ssm_p, cnew_p[:, first:],
            kv4(k_s, n_dec, dec_seq), kv4(v_s, n_dec, dec_seq),
            ssm_s, cnew_s[:, first:])
```

```python
import functools

import jax
import jax.numpy as jnp
from jax import lax
from jax.experimental import pallas as pl
from jax.experimental.pallas import tpu as pltpu

F32, BF16 = jnp.float32, jnp.bfloat16

D_MODEL = 4096
HEAD_DIM = 128
N_HEADS = 32
KV_HEADS = 8
KV_REP = 4
MOBA_BLOCK = 256
MOBA_TOPK = 3
ROPE_THETA = 10000.0
PAGE_SIZE = 128
SSM_HEADS = 64
SSM_HEAD_DIM = 64
SSM_STATE = 128
SSM_GROUPS = 8
SSM_REP = 8
SSM_INNER = 4096
SSM_CHUNK = 128
CONV_W = 4
CONV_DIM = 6144
D_FF = 16384
EPS = 1e-6
Q_DIM = N_HEADS * HEAD_DIM
KV_DIM = KV_HEADS * HEAD_DIM
ATTN_SCALE = HEAD_DIM ** -0.5
MASK_SCORE = -1e30
VMEM_LIMIT_BYTES = 60 * 1024 * 1024
LANES = 128
DEC_ROWS = 128
ROW_TILES = 5
PROJ_TN = 512
GATED_ROW_TILES = 8
DOWN_ROW_TILES = 4


def _params(n_axes):
    return pltpu.CompilerParams(dimension_semantics=("arbitrary",) * n_axes,
                                vmem_limit_bytes=VMEM_LIMIT_BYTES)


def _split3(x):
    hi = x.astype(BF16)
    r1 = x - hi.astype(F32)
    mid = r1.astype(BF16)
    lo = (r1 - mid.astype(F32)).astype(BF16)
    return hi, mid, lo


def _dot3_lhs_exact(w_exact, x):
    hi, mid, lo = _split3(x)
    f = functools.partial(jnp.dot, preferred_element_type=F32)
    return f(w_exact, lo) + f(w_exact, mid) + f(w_exact, hi)


def _dot3_rhs_exact(x, w_exact):
    hi, mid, lo = _split3(x)
    f = functools.partial(jnp.dot, preferred_element_type=F32)
    return f(lo, w_exact) + f(mid, w_exact) + f(hi, w_exact)


def _dot_hp(a, b, dims):
    a0, a1, a2 = _split3(a)
    b0, b1, b2 = _split3(b)
    f = functools.partial(lax.dot_general, dimension_numbers=dims, preferred_element_type=F32)
    return (f(a2, b0) + f(a1, b1) + f(a0, b2)) + (f(a1, b0) + f(a0, b1)) + f(a0, b0)


def _norm_rope(xh, g, c, s):
    xn = xh * lax.rsqrt(jnp.mean(xh * xh, axis=-1, keepdims=True) + EPS) * g
    return xn * c + pltpu.roll(xn, HEAD_DIM // 2, 1) * s


def _topk_select(gate, n_idx, n_blocks):
    cnt = jnp.zeros(gate.shape, F32)
    for m in range(n_blocks):
        rm = gate[m:m + 1, :]
        beats = jnp.where(rm > gate, 1.0, jnp.where((rm == gate) & (n_idx > m), 1.0, 0.0))
        cnt = cnt + beats
    return cnt < MOBA_TOPK


def _rope_tables(pos):
    half = HEAD_DIM // 2
    inv = ROPE_THETA ** (-jnp.arange(half, dtype=F32) / half)
    ang = pos.astype(F32)[:, None] * inv[None, :]
    c, s = jnp.cos(ang), jnp.sin(ang)
    return jnp.concatenate([c, c], axis=-1), jnp.concatenate([-s, s], axis=-1)


def _row_segments(tm, n_tiles, seq, n_batch, n_s):
    n_p = seq * n_batch
    tiles = []
    for t in range(n_tiles):
        lo, hi, cur = t * tm, (t + 1) * tm, []
        for b in range(n_batch):
            a, e = max(lo, b * seq), min(hi, (b + 1) * seq)
            if a < e:
                cur.append((a - lo, e - lo, "p", b))
        a, e = max(lo, n_p), min(hi, n_p + n_s)
        if a < e:
            assert (a, e) == (n_p, n_p + n_s) and (a - lo) % 8 == 0
            cur.append((a - lo, e - lo, "s", 0))
        tiles.append(cur)
    return tiles


def _per_tile(i, tiles, fn):
    for t, segs in enumerate(tiles):
        pl.when(i == t)(functools.partial(fn, segs))


def _epi_store(acc, i, ex, outs):
    outs[0][...] = acc.astype(outs[0].dtype)


def _epi_bias(acc, i, ex, outs):
    outs[0][...] = acc + ex[0][...]


def _epi_sigmoid(acc, i, ex, outs):
    outs[0][...] = jax.nn.sigmoid(acc)


def _epi_relu2(acc, i, ex, outs):
    r = jnp.maximum(acc, 0.0)
    outs[0][...] = (r * r).astype(outs[0].dtype)


def _epi_gate(acc, i, ex, outs):
    outs[0][...] = ex[0][...] * acc


def _epi_gate_add(acc, i, ex, outs):
    outs[0][...] = (ex[0][...] + ex[1][...] * acc).astype(outs[0].dtype)


def _make_epi_residual(tm, seq, n_batch):
    n_p = seq * n_batch

    def epi(acc, i, ex, outs):
        xp_ref, xs_ref, gp_ref, gs_ref = ex
        grow = i * tm + lax.broadcasted_iota(jnp.int32, acc.shape, 0)
        gate = gp_ref[n_batch - 1]
        for b in range(n_batch - 2, -1, -1):
            gate = jnp.where(grow < (b + 1) * seq, gp_ref[b], gate)
        pad = jnp.zeros((tm - DEC_ROWS, acc.shape[1]), F32)
        is_sample = grow >= n_p
        gate = jnp.where(is_sample, jnp.concatenate([pad, gs_ref[...]], axis=0), gate)
        x = jnp.where(is_sample, jnp.concatenate([pad, xs_ref[...]], axis=0), xp_ref[...])
        outs[0][...] = x + gate * acc

    return epi


def _mm_kernel(a_ref, w_ref, *refs, n_ex, epi, shift, w_is_t):
    if shift:
        w = jnp.concatenate([w_ref[shift:, :], refs[0][...]], axis=0)
        refs = refs[1:]
    else:
        w = w_ref[...]
    dims = (((1,), (1,)), ((), ())) if w_is_t else (((1,), (0,)), ((), ()))
    acc = lax.dot_general(a_ref[...], w.astype(BF16), dims, preferred_element_type=F32)
    epi(acc, pl.program_id(0), refs[:n_ex], refs[n_ex:])


def _mm(a, w, *, n_cols, col_off=0, shift=0, w_is_t=False, n_i, tn, out_dtype, epi, extras=(), extra_specs=(),
        name):
    m, kdim = a.shape
    tm = m // n_i
    assert tm * n_i == m and col_off % tn == 0 and n_cols % tn == 0 and (not shift or w_is_t)
    off = col_off // tn
    if w_is_t:
        w_spec = pl.BlockSpec((tn, kdim), lambda i, j: (j + off, 0))
    else:
        w_spec = pl.BlockSpec((kdim, tn), lambda i, j: (0, j + off))
    in_specs = [pl.BlockSpec((tm, kdim), lambda i, j: (i, 0)), w_spec]
    args = [a, w]
    if shift:
        assert shift % 8 == 0 and tn % shift == 0
        per = tn // shift
        in_specs.append(pl.BlockSpec((shift, kdim), lambda i, j: ((j + off + 1) * per, 0)))
        args.append(w)
    kern = functools.partial(_mm_kernel, n_ex=len(extras), epi=epi, shift=shift, w_is_t=w_is_t)
    return pl.pallas_call(
        kern, grid=(n_i, n_cols // tn), in_specs=in_specs + list(extra_specs),
        out_specs=pl.BlockSpec((tm, tn), lambda i, j: (i, j)),
        out_shape=jax.ShapeDtypeStruct((m, n_cols), out_dtype),
        compiler_params=_params(2), name=name)(*args, *extras)


def _ffn_down_kernel(a_ref, w_ref, x_ref, gp_ref, gs_ref, yp_ref, ys_ref, *, nk, tiles):
    i, k = pl.program_id(0), pl.program_id(2)
    @pl.when(k == 0)
    def _():
        yp_ref[...] = jnp.zeros(yp_ref.shape, F32)

    yp_ref[...] += jnp.dot(a_ref[...], w_ref[...].astype(BF16), preferred_element_type=F32)

    @pl.when(k == nk - 1)
    def _():
        def write(segs):
            for r0, r1, kind, b in segs:
                if kind == "s":
                    ys_ref[...] = x_ref[r0:r1, :] + gs_ref[...] * yp_ref[r0:r1, :]
            for r0, r1, kind, b in segs:
                if kind == "p":
                    yp_ref[r0:r1, :] = x_ref[r0:r1, :] + gp_ref[b] * yp_ref[r0:r1, :]

        _per_tile(i, tiles, write)


def _ffn_down(u, w, x, gate_p, gate_s, *, seq, n_batch, tn, tk):
    m, kdim = u.shape
    n_i = DOWN_ROW_TILES
    tm, nk, n_j = m // n_i, kdim // tk, D_MODEL // tn
    n_p = seq * n_batch
    tiles = _row_segments(tm, n_i, seq, n_batch, m - n_p)
    kern = functools.partial(_ffn_down_kernel, nk=nk, tiles=tiles)
    return pl.pallas_call(
        kern, grid=(n_i, n_j, nk),
        in_specs=[pl.BlockSpec((tm, tk), lambda i, j, k: (i, k)),
                  pl.BlockSpec((tk, tn), lambda i, j, k: (k, j)),
                  pl.BlockSpec((tm, tn), lambda i, j, k: (i, j)),
                  pl.BlockSpec((n_batch, 1, tn), lambda i, j, k: (0, 0, j)),
                  pl.BlockSpec((DEC_ROWS, tn), lambda i, j, k: (0, j))],
        out_specs=[pl.BlockSpec((tm, tn), lambda i, j, k: (i, j)),
                   pl.BlockSpec((DEC_ROWS, tn), lambda i, j, k: (0, jnp.where(i == n_i - 1, j, 0)))],
        out_shape=[jax.ShapeDtypeStruct((n_p, D_MODEL), F32), jax.ShapeDtypeStruct((m - n_p, D_MODEL), F32)],
        compiler_params=_params(3), name="ffn_down")(u, w, x, gate_p, gate_s)


def _place_kernel(big_ref, small_ref, o_ref):
    o_ref[...] = small_ref[...]


def _place_rows(big, small, row_block, name):
    return pl.pallas_call(
        _place_kernel, grid=(1,),
        in_specs=[pl.BlockSpec(memory_space=pl.ANY), pl.BlockSpec(small.shape, lambda i: (0, 0))],
        out_specs=pl.BlockSpec(small.shape, lambda i: (row_block, 0)),
        out_shape=jax.ShapeDtypeStruct(big.shape, big.dtype),
        input_output_aliases={0: 0}, compiler_params=_params(1), name=name)(big, small)


NORM_ROWS = 512


def _norm_mod_kernel(xp_ref, xs_ref, g_ref, scp_ref, shp_ref, scs_ref, shs_ref, o_ref, *, n_i):
    i = pl.program_id(0)

    def apply(x, sc, sh):
        y = x * lax.rsqrt(jnp.mean(x * x, axis=-1, keepdims=True) + EPS) * g_ref[...]
        return (y * (1.0 + sc) + sh).astype(o_ref.dtype)

    @pl.when(i < n_i - 1)
    def _():
        o_ref[...] = apply(xp_ref[...], scp_ref[0], shp_ref[0])

    @pl.when(i == n_i - 1)
    def _():
        o_ref[:DEC_ROWS, :] = apply(xs_ref[...], scs_ref[...], shs_ref[...])


def _norm_mod(x_p, x_s, s_block, g, sc_p, sh_p, sc_s, sh_s, *, n_p, rows_per_batch, name):
    d = x_p.shape[1]
    tm = NORM_ROWS
    n_i = n_p // tm + 1
    tpb = rows_per_batch // tm
    nb = sc_p.shape[0]
    pspec = pl.BlockSpec((1, 1, d), lambda i: (jnp.minimum(i // tpb, nb - 1), 0, 0))
    sspec = pl.BlockSpec((DEC_ROWS, d), lambda i: (0, 0))
    return pl.pallas_call(
        functools.partial(_norm_mod_kernel, n_i=n_i), grid=(n_i,),
        in_specs=[pl.BlockSpec((tm, d), lambda i: (jnp.minimum(i, n_i - 2), 0)),
                  pl.BlockSpec((DEC_ROWS, d), lambda i: (s_block, 0)),
                  pl.BlockSpec((1, d), lambda i: (0, 0)), pspec, pspec, sspec, sspec],
        out_specs=pl.BlockSpec((tm, d), lambda i: (i, 0)),
        out_shape=jax.ShapeDtypeStruct((n_p + DEC_ROWS, d), BF16),
        compiler_params=_params(1), name=name)(x_p, x_s, g, sc_p, sh_p, sc_s, sh_s)


def _norm_rope_kernel(x_ref, c_ref, s_ref, g_ref, o_ref, *, n_heads):
    c, s, g = c_ref[...], s_ref[...], g_ref[...]
    for h in range(n_heads):
        sl = slice(h * HEAD_DIM, (h + 1) * HEAD_DIM)
        o_ref[:, sl] = _norm_rope(x_ref[:, sl], g, c, s)


def _norm_rope_rows(x, row_block, c, s, g, *, n_heads, name):
    w = n_heads * HEAD_DIM
    return pl.pallas_call(
        functools.partial(_norm_rope_kernel, n_heads=n_heads), grid=(1,),
        in_specs=[pl.BlockSpec((DEC_ROWS, w), lambda i: (row_block, 0)),
                  pl.BlockSpec((DEC_ROWS, HEAD_DIM), lambda i: (0, 0)),
                  pl.BlockSpec((DEC_ROWS, HEAD_DIM), lambda i: (0, 0)),
                  pl.BlockSpec((1, HEAD_DIM), lambda i: (0, 0))],
        out_specs=pl.BlockSpec((DEC_ROWS, w), lambda i: (0, 0)),
        out_shape=jax.ShapeDtypeStruct((DEC_ROWS, w), F32),
        compiler_params=_params(1), name=name)(x, c, s, g)


def _k_prep_kernel(k_ref, v_ref, c_ref, s_ref, g_ref, ko_ref, ka_ref, va_ref, km_ref, *, n_blk):
    blk = pl.program_id(0) % n_blk
    c, s, g = c_ref[...], s_ref[...], g_ref[...]
    lane = lax.broadcasted_iota(jnp.int32, (MOBA_BLOCK, HEAD_DIM), 1)
    block_tag = jnp.where(lane == blk, 1.0, 0.0).astype(BF16)
    ones_col = jnp.where(lane == 0, 1.0, 0.0).astype(BF16)
    for h in range(KV_HEADS):
        sl = slice(h * HEAD_DIM, (h + 1) * HEAD_DIM)
        lo, mid, hi = 2 * h * HEAD_DIM, (2 * h + 1) * HEAD_DIM, (2 * h + 2) * HEAD_DIM
        kr = _norm_rope(k_ref[:, sl], g, c, s)
        ko_ref[:, sl] = kr
        ka_ref[:, lo:mid] = kr.astype(BF16)
        ka_ref[:, mid:hi] = block_tag
        va_ref[:, lo:mid] = v_ref[:, sl].astype(BF16)
        va_ref[:, mid:hi] = ones_col
        km_ref[0, :, sl] = jnp.mean(kr, axis=0, keepdims=True)


def _k_prep(kv, c, s, g, *, n_rows, n_blk):
    nt = n_rows // MOBA_BLOCK
    return pl.pallas_call(
        functools.partial(_k_prep_kernel, n_blk=n_blk), grid=(nt,),
        in_specs=[pl.BlockSpec((MOBA_BLOCK, KV_DIM), lambda t: (t, 0)),
                  pl.BlockSpec((MOBA_BLOCK, KV_DIM), lambda t: (t, 1)),
                  pl.BlockSpec((MOBA_BLOCK, HEAD_DIM), lambda t: (t % n_blk, 0)),
                  pl.BlockSpec((MOBA_BLOCK, HEAD_DIM), lambda t: (t % n_blk, 0)),
                  pl.BlockSpec((1, HEAD_DIM), lambda t: (0, 0))],
        out_specs=[pl.BlockSpec((MOBA_BLOCK, KV_DIM), lambda t: (t, 0)),
                   pl.BlockSpec((MOBA_BLOCK, 2 * KV_DIM), lambda t: (t, 0)),
                   pl.BlockSpec((MOBA_BLOCK, 2 * KV_DIM), lambda t: (t, 0)),
                   pl.BlockSpec((1, 1, KV_DIM), lambda t: (t, 0, 0))],
        out_shape=[jax.ShapeDtypeStruct((n_rows, KV_DIM), F32),
                   jax.ShapeDtypeStruct((n_rows, 2 * KV_DIM), BF16),
                   jax.ShapeDtypeStruct((n_rows, 2 * KV_DIM), BF16),
                   jax.ShapeDtypeStruct((nt, 1, KV_DIM), F32)],
        compiler_params=_params(1), name="k_prep")(kv, kv, c, s, g)


def _q_prep_kernel(q_ref, c_ref, s_ref, g_ref, km_ref, o_ref, *, n_blk):
    own = pl.program_id(0) % n_blk
    c, s, g = c_ref[...], s_ref[...], g_ref[...]
    n_idx = lax.broadcasted_iota(jnp.int32, (n_blk, MOBA_BLOCK), 0)
    lane = lax.broadcasted_iota(jnp.int32, (MOBA_BLOCK, HEAD_DIM), 1)
    past = n_idx < own
    km = km_ref[0]
    pad = jnp.zeros((HEAD_DIM - n_blk, MOBA_BLOCK), F32)
    for h in range(N_HEADS):
        kv = h // KV_REP
        qh = _norm_rope(q_ref[:, h * HEAD_DIM:(h + 1) * HEAD_DIM], g, c, s)
        gate = _dot_hp(km[:, kv * HEAD_DIM:(kv + 1) * HEAD_DIM], qh, (((1,), (1,)), ((), ())))
        gate = jnp.where(past, gate, -jnp.inf)
        sel = (_topk_select(gate, n_idx, n_blk) & past) | (n_idx == own)
        sel_t = jnp.concatenate([jnp.where(sel, 1.0, 0.0), pad], axis=0).T
        bias = jnp.where(lane < n_blk, (sel_t - 1.0) * (-MASK_SCORE), 0.0)
        o_ref[:, 2 * h * HEAD_DIM:(2 * h + 1) * HEAD_DIM] = (qh * ATTN_SCALE).astype(BF16)
        o_ref[:, (2 * h + 1) * HEAD_DIM:(2 * h + 2) * HEAD_DIM] = bias.astype(BF16)


def _q_prep(q, c, s, g, kmean, *, n_rows, n_blk):
    nt = n_rows // MOBA_BLOCK
    return pl.pallas_call(
        functools.partial(_q_prep_kernel, n_blk=n_blk), grid=(nt,),
        in_specs=[pl.BlockSpec((MOBA_BLOCK, Q_DIM), lambda t: (t, 0)),
                  pl.BlockSpec((MOBA_BLOCK, HEAD_DIM), lambda t: (t % n_blk, 0)),
                  pl.BlockSpec((MOBA_BLOCK, HEAD_DIM), lambda t: (t % n_blk, 0)),
                  pl.BlockSpec((1, HEAD_DIM), lambda t: (0, 0)),
                  pl.BlockSpec((1, n_blk, KV_DIM), lambda t: (t // n_blk, 0, 0))],
        out_specs=pl.BlockSpec((MOBA_BLOCK, 2 * Q_DIM), lambda t: (t, 0)),
        out_shape=jax.ShapeDtypeStruct((n_rows, 2 * Q_DIM), BF16),
        compiler_params=_params(1), name="q_prep")(q, c, s, g, kmean)


KEY_GROUP = 4 * MOBA_BLOCK


def _attn_p_kernel(q_ref, k_ref, v_ref, o_ref, m_ref, acc_ref):
    i = pl.program_id(2)
    rows = KV_REP * MOBA_BLOCK
    aug = 2 * HEAD_DIM
    qs = jnp.concatenate([q_ref[:, r * aug:(r + 1) * aug] for r in range(KV_REP)], axis=0)
    dims = (((1,), (1,)), ((), ()))
    blocks_per_group = KEY_GROUP // MOBA_BLOCK
    n_full = i // blocks_per_group
    group_start = pl.multiple_of(n_full * KEY_GROUP, KEY_GROUP)

    def own_group(n_blocks):
        width = n_blocks * MOBA_BLOCK
        kg, vg = k_ref[pl.ds(group_start, width), :], v_ref[pl.ds(group_start, width), :]
        n_chains = 1 if n_blocks == 1 else 2
        part = rows // n_chains
        qpos = (n_blocks - 1) * MOBA_BLOCK + (lax.broadcasted_iota(jnp.int32, (part, width), 0) & (MOBA_BLOCK - 1))
        visible = lax.broadcasted_iota(jnp.int32, (part, width), 1) <= qpos
        for c in range(n_chains):
            sl = slice(c * part, (c + 1) * part)
            s = jnp.where(visible, lax.dot_general(qs[sl], kg, dims, preferred_element_type=F32), -jnp.inf)
            m = jnp.max(s, axis=1, keepdims=True)
            m_ref[sl, :] = jnp.broadcast_to(m, (part, LANES))
            acc_ref[sl, :] = jnp.dot(jnp.exp(s - m).astype(BF16), vg, preferred_element_type=F32)

    half = rows // 2
    for n_blocks in range(1, blocks_per_group + 1):
        pl.when(i % blocks_per_group == n_blocks - 1)(functools.partial(own_group, n_blocks))

    def body(gi, carry):
        st = pl.multiple_of(gi * KEY_GROUP, KEY_GROUP)
        kg, vg = k_ref[pl.ds(st, KEY_GROUP), :], v_ref[pl.ds(st, KEY_GROUP), :]
        out = []
        for hh in range(2):
            m, acc = carry[2 * hh], carry[2 * hh + 1]
            s = lax.dot_general(qs[hh * half:(hh + 1) * half], kg, dims, preferred_element_type=F32)
            m_new = jnp.maximum(m, jnp.max(s, axis=1, keepdims=True))
            p = jnp.exp(s - jnp.tile(m_new, (1, KEY_GROUP // LANES)))
            pv = jnp.dot(p.astype(BF16), vg, preferred_element_type=F32)
            out += [m_new, acc * jnp.tile(jnp.exp(m - m_new), (1, aug // LANES)) + pv]
        return tuple(out)

    init = (m_ref[:half, :], acc_ref[:half, :], m_ref[half:, :], acc_ref[half:, :])
    fin = lax.fori_loop(0, n_full, body, init)
    acc = jnp.concatenate([fin[1], fin[3]], axis=0)
    o = acc[:, :HEAD_DIM] / acc[:, HEAD_DIM:HEAD_DIM + 1]
    for r in range(KV_REP):
        o_ref[:, r * HEAD_DIM:(r + 1) * HEAD_DIM] = o[r * MOBA_BLOCK:(r + 1) * MOBA_BLOCK].astype(o_ref.dtype)


def _attn_prompt(q_aug, k_aug, v_aug, *, n_batch, seq, out_rows):
    n_blk = seq // MOBA_BLOCK
    aug = 2 * HEAD_DIM
    return pl.pallas_call(
        _attn_p_kernel, grid=(n_batch, KV_HEADS, n_blk),
        in_specs=[pl.BlockSpec((MOBA_BLOCK, KV_REP * aug), lambda b, g, i: (b * n_blk + i, g)),
                  pl.BlockSpec((seq, aug), lambda b, g, i: (b, g)),
                  pl.BlockSpec((seq, aug), lambda b, g, i: (b, g))],
        out_specs=pl.BlockSpec((MOBA_BLOCK, KV_REP * HEAD_DIM), lambda b, g, i: (b * n_blk + i, g)),
        out_shape=jax.ShapeDtypeStruct((out_rows, Q_DIM), BF16),
        scratch_shapes=[pltpu.VMEM((KV_REP * MOBA_BLOCK, LANES), F32), pltpu.VMEM((KV_REP * MOBA_BLOCK, aug), F32)],
        compiler_params=_params(3), name="attn_prompt")(q_aug, k_aug, v_aug)


GROUP_ROWS = 16
PAGES_PER_STEP = 8


def _diag_blocks(full):
    return jnp.concatenate([full[g * GROUP_ROWS:(g + 1) * GROUP_ROWS, g * HEAD_DIM:(g + 1) * HEAD_DIM]
                            for g in range(KV_HEADS)], axis=0)


def _page_tile(ref):
    return jnp.concatenate([ref[0, pl.ds(g, PAGE_SIZE, stride=KV_HEADS), :] for g in range(KV_HEADS)], axis=1)


def _attn_s_kernel(pt_ref, *refs, n_blk, dec_seq):
    n_pg = PAGES_PER_STEP
    k_refs, v_refs = refs[:n_pg], refs[n_pg:2 * n_pg]
    qb_ref, qf_ref, ko_ref, vo_ref, o_ref, ob_ref, m_ref, l_ref, ks_ref = refs[2 * n_pg:]
    step = pl.program_id(1)
    pages_per_block = MOBA_BLOCK // PAGE_SIZE
    blocks_per_step = n_pg // pages_per_block
    qb = qb_ref[0]
    for u in range(blocks_per_step):
        jb = step * blocks_per_step + u
        pages = range(u * pages_per_block, (u + 1) * pages_per_block)
        kblk = jnp.concatenate([_page_tile(k_refs[p]) for p in pages], axis=0)
        vblk = jnp.concatenate([_page_tile(v_refs[p]) for p in pages], axis=0).astype(BF16)
        ks_ref[pl.ds(jb, 1), :] = jnp.sum(kblk, axis=0, keepdims=True)
        st = jnp.dot(kblk.astype(BF16), qb, preferred_element_type=F32)
        mb = jnp.max(st, axis=0, keepdims=True)
        p = jnp.exp(st - mb)
        m_ref[pl.ds(jb, 1), :] = mb
        l_ref[pl.ds(jb, 1), :] = jnp.sum(p, axis=0, keepdims=True)
        ob_ref[jb] = _diag_blocks(jnp.dot(p.T.astype(BF16), vblk, preferred_element_type=F32))

    @pl.when(step == n_blk // blocks_per_step - 1)
    def _():
        gate = _dot_hp(ks_ref[...] * (1.0 / MOBA_BLOCK), qf_ref[0], (((1,), (0,)), ((), ())))
        n_idx = lax.broadcasted_iota(jnp.int32, (n_blk, LANES), 0)
        sel = _topk_select(gate, n_idx, n_blk)
        m_all = m_ref[...]
        so = jnp.dot(ko_ref[0].astype(BF16), qb, preferred_element_type=F32)
        krow = lax.broadcasted_iota(jnp.int32, so.shape, 0)
        tq = (lax.broadcasted_iota(jnp.int32, so.shape, 1) >> 2) & (dec_seq - 1)
        so = jnp.where((krow <= tq) & (krow < dec_seq), so, -jnp.inf)
        m_tot = jnp.maximum(jnp.max(jnp.where(sel, m_all, -jnp.inf), axis=0, keepdims=True),
                            jnp.max(so, axis=0, keepdims=True))
        w = jnp.where(sel, jnp.exp(m_all - m_tot), 0.0)
        po = jnp.exp(so - m_tot)
        inv = 1.0 / (jnp.sum(w * l_ref[...], axis=0, keepdims=True) + jnp.sum(po, axis=0, keepdims=True))
        pad = jnp.zeros((LANES - n_blk - so.shape[0], LANES), F32)
        wt = jnp.concatenate([w * inv, po * inv, pad], axis=0).T
        acc = jnp.zeros((LANES, HEAD_DIM), F32)
        for b in range(n_blk):
            acc = acc + wt[:, b:b + 1] * ob_ref[b]
        vo = vo_ref[0]
        for t in range(dec_seq):
            vrow = jnp.concatenate(
                [jnp.broadcast_to(vo[t:t + 1, g * HEAD_DIM:(g + 1) * HEAD_DIM], (GROUP_ROWS, HEAD_DIM))
                 for g in range(KV_HEADS)], axis=0)
            acc = acc + wt[:, n_blk + t:n_blk + t + 1] * vrow
        o_ref[0] = acc


def _attn_sample(page_table, cache_k, cache_v, qbd, qbd_f32, k_own, v_own, *, dec_seq):
    n_seq, n_pages = page_table.shape
    n_blk = n_pages * PAGE_SIZE // MOBA_BLOCK
    page_rows = PAGE_SIZE * KV_HEADS
    n_pg = PAGES_PER_STEP
    assert n_pages % n_pg == 0
    pages = [pl.BlockSpec((1, page_rows, HEAD_DIM), lambda s, t, pt, p=p: (pt[s, t * n_pg + p], 0, 0))
             for p in range(n_pg)]
    per_seq = lambda r, c: pl.BlockSpec((1, r, c), lambda s, t, pt: (s, 0, 0))
    grid_spec = pltpu.PrefetchScalarGridSpec(
        num_scalar_prefetch=1, grid=(n_seq, n_pages // n_pg),
        in_specs=pages + pages + [per_seq(KV_DIM, LANES), per_seq(KV_DIM, LANES),
                                  per_seq(8, KV_DIM), per_seq(8, KV_DIM)],
        out_specs=per_seq(LANES, HEAD_DIM),
        scratch_shapes=[pltpu.VMEM((n_blk, LANES, HEAD_DIM), F32), pltpu.VMEM((n_blk, LANES), F32),
                        pltpu.VMEM((n_blk, LANES), F32), pltpu.VMEM((n_blk, KV_DIM), F32)])
    return pl.pallas_call(
        functools.partial(_attn_s_kernel, n_blk=n_blk, dec_seq=dec_seq), grid_spec=grid_spec,
        out_shape=jax.ShapeDtypeStruct((n_seq, LANES, HEAD_DIM), F32),
        compiler_params=_params(2), name="attn_sample")(
            page_table, *([cache_k] * n_pg), *([cache_v] * n_pg), qbd, qbd_f32, k_own, v_own)


CONV_PAD = 8
SHORT_CHUNK = 16


def _chunk_t(x, L):
    if L == LANES:
        return x.T
    return jnp.concatenate([x, jnp.zeros((LANES - L, LANES), F32)], axis=0).T[:, :L]


def _ssd_kernel(xbc_ref, z_ref, dt_ref, cprev_ref, h0_ref, wc_ref, bc_ref, dtb_ref, alog_ref, dx_ref, gs_ref,
                e2_ref, e1_ref, tril_ref, y_ref, hf_ref, cnew_ref, ht_ref, cbuf_ref, *, L, nc, n_valid, rows_in):
    c = pl.program_id(1)
    gw = SSM_REP * SSM_HEAD_DIM

    @pl.when(c == 0)
    def _():
        cbuf_ref[0:CONV_PAD, :] = cprev_ref[0]
        if rows_in < L:
            cbuf_ref[CONV_PAD + rows_in:, :] = jnp.zeros((L - rows_in, CONV_DIM), F32)
        for g in range(SSM_GROUPS):
            ht_ref[g] = h0_ref[0, g * SSM_REP:(g + 1) * SSM_REP].reshape(gw, SSM_STATE).T

    cbuf_ref[CONV_PAD:CONV_PAD + rows_in, :] = xbc_ref[...]
    first = CONV_PAD - (CONV_W - 1)
    window = cbuf_ref[...]
    taps = window * wc_ref[0:1, :]
    for j in range(1, CONV_W):
        taps = pltpu.roll(taps, 1, 0) + window * wc_ref[j:j + 1, :]
    conv = bc_ref[...] + taps[CONV_PAD:, :]
    cnew_ref[0] = cbuf_ref[n_valid:CONV_PAD + n_valid, :]
    cbuf_ref[first:CONV_PAD, :] = cbuf_ref[first + L:CONV_PAD + L, :]

    act = conv * jax.nn.sigmoid(conv)
    xs = act[:, :SSM_INNER]
    if rows_in < L:
        zpad = jnp.zeros((L - rows_in, SSM_INNER), F32)
        z = jnp.concatenate([z_ref[...], zpad], axis=0)
        dt_raw = jnp.concatenate([dt_ref[...], jnp.zeros((L - rows_in, LANES), F32)], axis=0)
    else:
        z, dt_raw = z_ref[...], dt_ref[...]

    row = lax.broadcasted_iota(jnp.int32, (L, LANES), 0)
    lane = lax.broadcasted_iota(jnp.int32, (L, LANES), 1)
    dt = jax.nn.softplus(dt_raw + dtb_ref[...])
    dt = jnp.where((row < n_valid) & (lane < SSM_HEADS), dt, 0.0)
    cum = _dot3_lhs_exact(tril_ref[...], dt * (-jnp.exp(alog_ref[...])))
    cum_t, dt_t = _chunk_t(cum, L), _chunk_t(dt, L)
    causal = lax.broadcasted_iota(jnp.int32, (L, L), 1) <= lax.broadcasted_iota(jnp.int32, (L, L), 0)

    outs = []
    for g in range(SSM_GROUPS):
        sl = slice(g * gw, (g + 1) * gw)
        bg = act[:, SSM_INNER + g * SSM_STATE:SSM_INNER + (g + 1) * SSM_STATE]
        cg = act[:, SSM_INNER + (SSM_GROUPS + g) * SSM_STATE:SSM_INNER + (SSM_GROUPS + g + 1) * SSM_STATE]
        bg_b, cg_b = bg.astype(BF16), cg.astype(BF16)
        cb = lax.dot_general(cg_b, bg_b, (((1,), (1,)), ((), ())), preferred_element_type=F32)
        cum_x2 = _dot3_rhs_exact(cum, e2_ref[:, g * SSM_REP * LANES:(g + 1) * SSM_REP * LANES])
        cum_x = _dot3_rhs_exact(cum, e1_ref[:, sl])
        dt_x = _dot3_rhs_exact(dt, e1_ref[:, sl])
        ht = ht_ref[g]
        y_inter = jnp.dot(cg_b, ht.astype(BF16), preferred_element_type=F32)
        parts = []
        for r in range(SSM_REP):
            h = g * SSM_REP + r
            seg = cum_x2[:, r * LANES:r * LANES + L] - cum_t[h:h + 1, :]
            w = cb * jnp.exp(jnp.where(causal, seg, -jnp.inf)) * dt_t[h:h + 1, :]
            xh = xs[:, h * SSM_HEAD_DIM:(h + 1) * SSM_HEAD_DIM].astype(BF16)
            parts.append(jnp.dot(w.astype(BF16), xh, preferred_element_type=F32))
        xg = xs[:, sl]
        yg = jnp.concatenate(parts, axis=1) + y_inter * jnp.exp(cum_x) + dx_ref[:, sl] * xg
        last = cum_x[L - 1:L, :]
        xw = (xg * (jnp.exp(last - cum_x) * dt_x)).astype(BF16)
        ht_ref[g] = ht * jnp.exp(last) + jnp.dot(_chunk_t(bg, L).astype(BF16), xw, preferred_element_type=F32)
        zg = z[:, sl]
        yg = yg * (zg * jax.nn.sigmoid(zg))
        yg = yg * lax.rsqrt(jnp.mean(yg * yg, axis=-1, keepdims=True) + EPS) * gs_ref[:, sl]
        outs.append(yg[:rows_in].astype(y_ref.dtype))
    y_ref[...] = jnp.concatenate(outs, axis=1)

    @pl.when(c == nc - 1)
    def _():
        for g in range(SSM_GROUPS):
            hf_ref[0, g * SSM_REP:(g + 1) * SSM_REP] = ht_ref[g].T.reshape(SSM_REP, SSM_HEAD_DIM, SSM_STATE)


def _ssd(xbc, z, dt, conv_prev, h0, w_conv, b_conv, dt_bias, a_log, d_skip, g_ssm, *, n_seq, nc, rows_in, n_valid,
         out_rows, chunk):
    L = chunk
    head_of = jnp.arange(SSM_HEADS * LANES, dtype=jnp.int32) // LANES
    e2 = (jnp.arange(LANES, dtype=jnp.int32)[:, None] == head_of[None, :]).astype(BF16)
    chan_head = jnp.arange(SSM_INNER, dtype=jnp.int32) // SSM_HEAD_DIM
    e1 = (jnp.arange(LANES, dtype=jnp.int32)[:, None] == chan_head[None, :]).astype(BF16)
    tril = (jnp.arange(L)[:, None] >= jnp.arange(L)[None, :]).astype(BF16)
    pad_l = lambda v: jnp.pad(v.reshape(1, -1), ((0, 0), (0, LANES - v.shape[0])))
    dx = jnp.repeat(d_skip, SSM_HEAD_DIM).reshape(1, SSM_INNER)
    const = lambda shape: pl.BlockSpec(shape, lambda b, c: (0,) * len(shape))
    kern = functools.partial(_ssd_kernel, L=L, nc=nc, n_valid=n_valid, rows_in=rows_in)
    return pl.pallas_call(
        kern, grid=(n_seq, nc),
        in_specs=[pl.BlockSpec((rows_in, CONV_DIM), lambda b, c: (b * nc + c, 0)),
                  pl.BlockSpec((rows_in, SSM_INNER), lambda b, c: (b * nc + c, 0)),
                  pl.BlockSpec((rows_in, LANES), lambda b, c: (b * nc + c, 0)),
                  pl.BlockSpec((1, CONV_PAD, CONV_DIM), lambda b, c: (b, 0, 0)),
                  pl.BlockSpec((1, SSM_HEADS, SSM_HEAD_DIM, SSM_STATE), lambda b, c: (b, 0, 0, 0)),
                  const((CONV_W, CONV_DIM)), const((1, CONV_DIM)), const((1, LANES)), const((1, LANES)),
                  const((1, SSM_INNER)), const((1, SSM_INNER)),
                  const((LANES, SSM_HEADS * LANES)), const((LANES, SSM_INNER)), const((L, L))],
        out_specs=[pl.BlockSpec((rows_in, SSM_INNER), lambda b, c: (b * nc + c, 0)),
                   pl.BlockSpec((1, SSM_HEADS, SSM_HEAD_DIM, SSM_STATE), lambda b, c: (b, 0, 0, 0)),
                   pl.BlockSpec((1, CONV_PAD, CONV_DIM), lambda b, c: (b, 0, 0))],
        out_shape=[jax.ShapeDtypeStruct((out_rows, SSM_INNER), BF16),
                   jax.ShapeDtypeStruct((n_seq, SSM_HEADS, SSM_HEAD_DIM, SSM_STATE), F32),
                   jax.ShapeDtypeStruct((n_seq, CONV_PAD, CONV_DIM), F32)],
        scratch_shapes=[pltpu.VMEM((SSM_GROUPS, SSM_STATE, SSM_REP * SSM_HEAD_DIM), F32),
                        pltpu.VMEM((CONV_PAD + L, CONV_DIM), F32)],
        compiler_params=_params(2), name="ssd")(
            xbc, z, dt, conv_prev, h0, w_conv, b_conv.reshape(1, -1), pad_l(dt_bias), pad_l(a_log), dx,
            g_ssm.reshape(1, -1), e2, e1, tril)


def kernel(x_prompt, x_sample, cache_k, cache_v, state_ssm, state_conv, page_table, c_prompt, c_sample, w_ada, b_ada, g_norm1, w_in, q_norm_g, k_norm_g, w_conv, b_conv, dt_bias, a_log, d_skip, g_ssm, w_attn_o, w_ssm_o, w_out, g_norm2, w_up, w_down):
    n_batch, seq, d = x_prompt.shape
    n_dec, dec_seq, _ = x_sample.shape
    n_p, n_s = n_batch * seq, n_dec * dec_seq
    n_rows = n_p + n_s
    assert n_s == DEC_ROWS and d == D_MODEL and seq % KEY_GROUP == 0 and n_p % DEC_ROWS == 0
    n_pool = cache_k.shape[0]
    past_len = page_table.shape[1] * PAGE_SIZE
    n_blk = seq // MOBA_BLOCK
    sample_block = n_p // DEC_ROWS
    xp2, xs2 = x_prompt.reshape(n_p, D_MODEL), x_sample.reshape(n_s, D_MODEL)
    row1 = lambda v: v.reshape(1, -1)

    c_all = jnp.concatenate([c_prompt, c_sample], axis=0)
    c_pad = jnp.pad(c_all, ((0, -c_all.shape[0] % 16), (0, 0))).astype(BF16)
    tn = PROJ_TN
    mod = _mm(c_pad, w_ada, n_cols=6 * D_MODEL, n_i=1, tn=tn, out_dtype=F32, epi=_epi_bias,
              extras=(row1(b_ada),), extra_specs=[pl.BlockSpec((1, tn), lambda i, j: (0, j))], name="ada_mod")
    mod = mod[:n_batch + n_dec].reshape(n_batch + n_dec, 6, D_MODEL)
    sh1, sc1, gt1, sh2, sc2, gt2 = (
        (mod[:n_batch, i].reshape(n_batch, 1, D_MODEL), jnp.repeat(mod[n_batch:, i], dec_seq, axis=0))
        for i in range(6))

    h = _norm_mod(xp2, xs2, 0, row1(g_norm1), sc1[0], sh1[0], sc1[1], sh1[1], n_p=n_p, rows_per_batch=seq,
                  name="norm_mod1")
    w_in_t = w_in.T
    proj = functools.partial(_mm, h, w_in_t, w_is_t=True, n_i=ROW_TILES, out_dtype=F32)
    q = proj(n_cols=Q_DIM, col_off=0, tn=tn, epi=_epi_store, name="in_q")
    kv = proj(n_cols=2 * KV_DIM, col_off=Q_DIM, tn=tn, epi=_epi_store, name="in_kv")
    z = proj(n_cols=SSM_INNER, col_off=Q_DIM + 2 * KV_DIM, tn=tn, epi=_epi_store, name="in_z")
    xbc = proj(n_cols=CONV_DIM, col_off=Q_DIM + 2 * KV_DIM + SSM_INNER, tn=tn, epi=_epi_store, name="in_xbc")
    dt_off = Q_DIM + 2 * KV_DIM + SSM_INNER + CONV_DIM
    dt = proj(n_cols=LANES, col_off=dt_off, tn=LANES, epi=_epi_store, name="in_dt")
    gates = proj(n_cols=2 * D_MODEL, col_off=dt_off, shift=SSM_HEADS, tn=tn, epi=_epi_sigmoid, name="in_gates")

    cp, sp = _rope_tables(jnp.arange(seq, dtype=jnp.int32))
    k_p, k_aug, v_aug, kmean = _k_prep(kv, cp, sp, row1(k_norm_g), n_rows=n_p, n_blk=n_blk)
    q_aug = _q_prep(q, cp, sp, row1(q_norm_g), kmean.reshape(n_batch, n_blk, KV_DIM), n_rows=n_p, n_blk=n_blk)
    attn = _attn_prompt(q_aug, k_aug, v_aug, n_batch=n_batch, seq=seq, out_rows=n_rows)

    pos_s = past_len + (jnp.arange(n_s, dtype=jnp.int32) % dec_seq)
    cs, ss = _rope_tables(pos_s)
    q_s = _norm_rope_rows(q, sample_block, cs, ss, row1(q_norm_g), n_heads=N_HEADS, name="q_sample")
    k_s = _norm_rope_rows(kv, sample_block, cs, ss, row1(k_norm_g), n_heads=KV_HEADS, name="k_sample")
    v_s = kv[n_p:, KV_DIM:]
    q_t = q_s.reshape(n_dec, dec_seq, KV_HEADS, KV_REP, HEAD_DIM).transpose(0, 2, 4, 1, 3)
    q_t = q_t.reshape(n_dec, KV_HEADS, HEAD_DIM, GROUP_ROWS)
    eye = jnp.eye(KV_HEADS, dtype=F32)
    qbd_f32 = (q_t[:, :, :, None, :] * eye[None, :, None, :, None]).reshape(n_dec, KV_DIM, LANES)
    qbd = (qbd_f32 * ATTN_SCALE).astype(BF16)
    own_pad = lambda v: jnp.pad(v.reshape(n_dec, dec_seq, KV_DIM), ((0, 0), (0, 8 - dec_seq), (0, 0)))
    pool_view = lambda cache: cache.reshape(n_pool, PAGE_SIZE * KV_HEADS, HEAD_DIM)
    o_s = _attn_sample(page_table, pool_view(cache_k), pool_view(cache_v), qbd, qbd_f32, own_pad(k_s), own_pad(v_s),
                       dec_seq=dec_seq)
    attn_s = o_s.reshape(n_dec, KV_HEADS, dec_seq, KV_REP, HEAD_DIM).transpose(0, 2, 1, 3, 4)
    attn = _place_rows(attn, attn_s.reshape(n_s, Q_DIM).astype(BF16), sample_block, "attn_place")

    ssd = functools.partial(_ssd, w_conv=w_conv, b_conv=b_conv, dt_bias=dt_bias, a_log=a_log, d_skip=d_skip,
                            g_ssm=g_ssm)
    conv_pad = lambda v: jnp.pad(v, ((0, 0), (CONV_PAD - (CONV_W - 1), 0), (0, 0)))
    zeros_h = jnp.zeros((n_batch, SSM_HEADS, SSM_HEAD_DIM, SSM_STATE), F32)
    zeros_c = jnp.zeros((n_batch, CONV_W - 1, CONV_DIM), F32)
    y_ssm, ssm_p, cnew_p = ssd(xbc, z, dt, conv_pad(zeros_c), zeros_h, n_seq=n_batch, nc=seq // SSM_CHUNK,
                               rows_in=SSM_CHUNK, n_valid=SSM_CHUNK, out_rows=n_rows, chunk=SSM_CHUNK)
    short = SHORT_CHUNK
    pad_s = lambda v: jnp.pad(v[n_p:].reshape(n_dec, dec_seq, -1), ((0, 0), (0, short - dec_seq), (0, 0))).reshape(
        n_dec * short, -1)
    y_s, ssm_s, cnew_s = ssd(pad_s(xbc), pad_s(z), pad_s(dt), conv_pad(state_conv), state_ssm, n_seq=n_dec, nc=1,
                             rows_in=short, n_valid=dec_seq, out_rows=n_dec * short, chunk=short)
    y_s = y_s.reshape(n_dec, short, SSM_INNER)[:, :dec_seq].reshape(n_s, SSM_INNER)
    y_ssm = _place_rows(y_ssm, y_s, sample_block, "ssm_place")

    n_i = GATED_ROW_TILES
    tm = n_rows // n_i
    tile = lambda off: pl.BlockSpec((tm, tn), lambda i, j: (i, j + off))
    mix_a = _mm(attn, w_attn_o, n_cols=D_MODEL, n_i=n_i, tn=tn, out_dtype=F32, epi=_epi_gate,
                extras=(gates,), extra_specs=[tile(0)], name="attn_proj")
    mixed = _mm(y_ssm, w_ssm_o, n_cols=D_MODEL, n_i=n_i, tn=tn, out_dtype=BF16, epi=_epi_gate_add,
                extras=(mix_a, gates), extra_specs=[tile(0), tile(D_MODEL // tn)], name="ssm_proj")
    assert n_rows - n_s == n_p and tm >= n_s
    res_specs = [tile(0), pl.BlockSpec((DEC_ROWS, tn), lambda i, j: (0, j)),
                 pl.BlockSpec((n_batch, 1, tn), lambda i, j: (0, 0, j)),
                 pl.BlockSpec((DEC_ROWS, tn), lambda i, j: (0, j))]
    x1 = _mm(mixed, w_out, n_cols=D_MODEL, n_i=n_i, tn=tn, out_dtype=F32, epi=_make_epi_residual(tm, seq, n_batch),
             extras=(xp2, xs2, gt1[0], gt1[1]), extra_specs=res_specs, name="out_proj")

    h2 = _norm_mod(x1, x1, sample_block, row1(g_norm2), sc2[0], sh2[0], sc2[1], sh2[1], n_p=n_p,
                   rows_per_batch=seq, name="norm_mod2")
    u = _mm(h2, w_up, n_cols=D_FF, n_i=ROW_TILES, tn=PROJ_TN, out_dtype=BF16, epi=_epi_relu2, name="ffn_up")
    y_p, y_s2 = _ffn_down(u, w_down, x1, gt2[0], gt2[1], seq=seq, n_batch=n_batch, tn=512, tk=2048)

    kv4 = lambda v, b, t: v.reshape(b, t, KV_HEADS, HEAD_DIM)
    first = CONV_PAD - (CONV_W - 1)
    return (y_p.reshape(n_batch, seq, D_MODEL), y_s2.reshape(n_dec, dec_seq, D_MODEL),
            kv4(k_p, n_batch, seq), kv4(kv[:n_p, KV_DIM:], n_batch, seq),
            ssm_p, cnew_p[:, first:],
            kv4(k_s, n_dec, dec_seq), kv4(v_s, n_dec, dec_seq),
            ssm_s, cnew_s[:, first:])
```
